```python
import jax, jax.numpy as jnp
from jax import lax
import numpy as np

D_MODEL = 1024
BATCH = 8
SEQ = 4096
DEPTH = 2
DEC_BATCH = 32
DEC_SEQ = 1
PAST_LEN = 16384
PAGE_SIZE = 128

N_HEADS = 8
N_KV_HEADS = 2
KV_GROUP = N_HEADS // N_KV_HEADS
ATTN_WIDTH = D_MODEL // 2
HEAD_DIM = ATTN_WIDTH // N_HEADS
ROPE_THETA = 500000.0
N_IDX_HEADS = 8
IDX_DIM = 64
TOPK_MAX = 256
Q_BLOCK = 128
POOL_WINDOWS = (2, 4, 8, 16)
N_POOL_GROUPS = len(POOL_WINDOWS)
POOL_WIDTH = D_MODEL - ATTN_WIDTH
POOL_GROUP_DIM = POOL_WIDTH // N_POOL_GROUPS
POOL_BUF = max(POOL_WINDOWS) - 1
D_FF = 2816
CONV_WIDTH = 3
RMS_EPS = 1e-6
SPLIT_SIZES = (ATTN_WIDTH, N_KV_HEADS * HEAD_DIM, N_KV_HEADS * HEAD_DIM,
               N_IDX_HEADS * IDX_DIM, IDX_DIM, N_IDX_HEADS, POOL_WIDTH)
IN_WIDTH = sum(SPLIT_SIZES)

kernel_name = "hymba_dsa_pool_convffn_step"


def rms_norm(x, g):
    xf = x.astype(jnp.float32)
    y = xf * lax.rsqrt(jnp.mean(xf * xf, axis=-1, keepdims=True) + RMS_EPS)
    return (y * g.astype(jnp.float32)).astype(x.dtype)


def partial_rope(x, pos):
    rot = x.shape[-1] // 4
    half = rot // 2
    inv = ROPE_THETA ** (-jnp.arange(half, dtype=jnp.float32) / half)
    ang = pos.astype(jnp.float32)[:, None] * inv[None, :]
    cos = jnp.cos(ang)[None, :, None, :]
    sin = jnp.sin(ang)[None, :, None, :]
    xr = x[..., :rot].astype(jnp.float32)
    x1, x2 = xr[..., :half], xr[..., half:]
    r = jnp.concatenate([x1 * cos - x2 * sin, x2 * cos + x1 * sin], axis=-1).astype(x.dtype)
    return jnp.concatenate([r, x[..., rot:]], axis=-1)


def project(h, w_in, pos):
    B, T = h.shape[:2]
    z = h @ w_in
    cuts = [int(c) for c in np.cumsum(SPLIT_SIZES)[:-1]]
    q, k, v, qi, ki, wi, xp = jnp.split(z, cuts, axis=-1)
    q = partial_rope(q.reshape(B, T, N_HEADS, HEAD_DIM), pos)
    k = partial_rope(k.reshape(B, T, N_KV_HEADS, HEAD_DIM), pos)
    v = v.reshape(B, T, N_KV_HEADS, HEAD_DIM)
    qi = partial_rope(qi.reshape(B, T, N_IDX_HEADS, IDX_DIM), pos)
    ki = partial_rope(ki.reshape(B, T, 1, IDX_DIM), pos)[:, :, 0]
    return q, k, v, qi, ki, wi, xp


def indexer_scores(qi, wi, ki, q_pos, k_pos):
    dots = jnp.einsum('bqhd,bld->bqhl', qi, ki) * (IDX_DIM ** -0.5)
    s = jnp.einsum('bqhl,bqh->bql', jax.nn.relu(dots), wi).astype(jnp.float32) * (N_IDX_HEADS ** -0.5)
    return jnp.where(k_pos[None, None, :] <= q_pos[None, :, None], s, -jnp.inf)


def sparse_attend(q, k_sel, v_sel, valid):
    B, Q = q.shape[:2]
    qg = q.reshape(B, Q, N_KV_HEADS, KV_GROUP, HEAD_DIM)
    s = jnp.einsum('bqngd,bqknd->bqngk', qg, k_sel).astype(jnp.float32) * (HEAD_DIM ** -0.5)
    s = jnp.where(valid[:, :, None, None, :], s, -jnp.inf)
    p = jax.nn.softmax(s, axis=-1).astype(v_sel.dtype)
    o = jnp.einsum('bqngk,bqknd->bqngd', p, v_sel)
    return o.reshape(B, Q, ATTN_WIDTH)


def prompt_attention(q, k, v, qi, ki, wi):
    B, S = q.shape[:2]
    topk = min(TOPK_MAX, S // 4)
    k_pos = jnp.arange(S)
    bidx = jnp.arange(B)[:, None, None]

    def block(i):
        q0 = i * Q_BLOCK
        qb = lax.dynamic_slice_in_dim(q, q0, Q_BLOCK, axis=1)
        qib = lax.dynamic_slice_in_dim(qi, q0, Q_BLOCK, axis=1)
        wib = lax.dynamic_slice_in_dim(wi, q0, Q_BLOCK, axis=1)
        q_pos = q0 + jnp.arange(Q_BLOCK)
        sc = indexer_scores(qib, wib, ki, q_pos, k_pos)
        _, sel = lax.top_k(sc, topk)
        valid = sel <= q_pos[None, :, None]
        return sparse_attend(qb, k[bidx, sel], v[bidx, sel], valid)

    out = lax.map(block, jnp.arange(S // Q_BLOCK))
    return out.transpose(1, 0, 2, 3).reshape(B, S, ATTN_WIDTH)


def sample_attention(q, k_new, v_new, qi, ki_new, wi, ck, cv, cki, page_table):
    B, T = q.shape[:2]
    past = page_table.shape[1] * PAGE_SIZE
    L = past + T
    topk = min(TOPK_MAX, L // 4)
    ki_past = cki[page_table].reshape(B, past, IDX_DIM)
    ki_all = jnp.concatenate([ki_past, ki_new], axis=1)
    q_pos = past + jnp.arange(T)
    sc = indexer_scores(qi, wi, ki_all, q_pos, jnp.arange(L))
    _, sel = lax.top_k(sc, topk)
    valid = sel <= q_pos[None, :, None]
    bidx = jnp.arange(B)[:, None, None]
    in_past = (sel < past)[..., None, None]
    sp = jnp.minimum(sel, past - 1)
    phys = page_table[bidx, sp // PAGE_SIZE]
    off = sp % PAGE_SIZE
    sn = jnp.clip(sel - past, 0, T - 1)
    k_sel = jnp.where(in_past, ck[phys, off], k_new[bidx, sn])
    v_sel = jnp.where(in_past, cv[phys, off], v_new[bidx, sn])
    return sparse_attend(q, k_sel, v_sel, valid)


def pool_mixer(x_ext, n_out, w_pool, pool_scale):
    B, R = x_ext.shape[:2]
    xg = x_ext.reshape(B, R, N_POOL_GROUPS, POOL_GROUP_DIM).astype(jnp.float32)
    cs0 = jnp.concatenate([jnp.zeros_like(xg[:, :1]), jnp.cumsum(xg, axis=1)], axis=1)
    idx = jnp.arange(R - n_out, R)
    win = jnp.asarray(POOL_WINDOWS, jnp.int32)
    lo = jnp.maximum(idx[:, None] + 1 - win[None, :], 0)
    gidx = jnp.arange(N_POOL_GROUPS)[None, :]
    wsum = cs0[:, idx + 1] - cs0[:, lo, gidx]
    count = (idx[:, None] + 1 - lo).astype(jnp.float32)
    diff = (wsum / count[None, :, :, None] - xg[:, R - n_out:]).astype(x_ext.dtype)
    out = jnp.einsum('btgc,gcd->btgd', diff, w_pool).reshape(B, n_out, POOL_WIDTH)
    return out * pool_scale


def conv_ffn(h, buf, w_gate, w_val, conv_w, conv_b, w_down):
    g = h @ w_gate
    T = g.shape[1]
    ext = jnp.concatenate([buf, g], axis=1)
    c = conv_b + sum(ext[:, j:j + T] * conv_w[j] for j in range(CONV_WIDTH))
    y = (jax.nn.gelu(c, approximate=True) * (h @ w_val)) @ w_down
    return y, ext[:, -(CONV_WIDTH - 1):]


def layer(x, pos, attn_fn, pool_prev, conv_prev, p):
    T = x.shape[1]
    h = rms_norm(x, p['g_pre_mix'])
    q, k, v, qi, ki, wi, xp = project(h, p['w_in'], pos)
    a = attn_fn(q, k, v, qi, ki, wi)
    pool_ext = jnp.concatenate([pool_prev, xp], axis=1)
    m = pool_mixer(pool_ext, T, p['w_pool'], p['pool_scale'])
    mix = jnp.concatenate([a, m], axis=-1) @ p['w_out']
    x = x + rms_norm(mix, p['g_post_mix'])
    h2 = rms_norm(x, p['g_pre_ffn'])
    f, conv_new = conv_ffn(h2, conv_prev, p['w_gate'], p['w_val'], p['conv_w'], p['conv_b'], p['w_down'])
    x = x + rms_norm(f, p['g_post_ffn'])
    return x, k, v, ki, pool_ext[:, -POOL_BUF:], conv_new


def setup_inputs(seed: int = 0) -> dict:
    key = jax.random.key(seed)
    ks = jax.random.split(key, 24)
    n_pages = PAST_LEN // PAGE_SIZE
    n_phys = (DEC_BATCH * n_pages * 5) // 4
    f32 = jnp.float32
    nrm = lambda k, shape, s=1.0: jax.random.normal(k, shape, f32) * s
    gain = lambda k: 1.0 + 0.05 * jax.random.normal(k, (DEPTH, D_MODEL), f32)
    page_table = jax.random.permutation(ks[7], n_phys)[:DEC_BATCH * n_pages].reshape(DEC_BATCH, n_pages).astype(jnp.int32)
    return {
        'x_prompt': nrm(ks[0], (BATCH, SEQ, D_MODEL)),
        'x_sample': nrm(ks[1], (DEC_BATCH, DEC_SEQ, D_MODEL)),
        'cache_k': nrm(ks[2], (DEPTH, n_phys, PAGE_SIZE, N_KV_HEADS, HEAD_DIM)),
        'cache_v': nrm(ks[3], (DEPTH, n_phys, PAGE_SIZE, N_KV_HEADS, HEAD_DIM)),
        'cache_idx_k': nrm(ks[4], (DEPTH, n_phys, PAGE_SIZE, IDX_DIM)),
        'state_pool': nrm(ks[5], (DEPTH, DEC_BATCH, POOL_BUF, POOL_WIDTH)),
        'state_conv': nrm(ks[6], (DEPTH, DEC_BATCH, CONV_WIDTH - 1, D_FF)),
        'page_table': page_table,
        'g_pre_mix': gain(ks[8]),
        'w_in': nrm(ks[9], (DEPTH, D_MODEL, IN_WIDTH), D_MODEL ** -0.5),
        'w_pool': nrm(ks[10], (DEPTH, N_POOL_GROUPS, POOL_GROUP_DIM, POOL_GROUP_DIM), POOL_GROUP_DIM ** -0.5),
        'pool_scale': 1.0 + 0.1 * nrm(ks[11], (DEPTH, POOL_WIDTH)),
        'w_out': nrm(ks[12], (DEPTH, D_MODEL, D_MODEL), D_MODEL ** -0.5),
        'g_post_mix': gain(ks[13]),
        'g_pre_ffn': gain(ks[14]),
        'w_gate': nrm(ks[15], (DEPTH, D_MODEL, D_FF), D_MODEL ** -0.5),
        'w_val': nrm(ks[16], (DEPTH, D_MODEL, D_FF), D_MODEL ** -0.5),
        'conv_w': nrm(ks[17], (DEPTH, CONV_WIDTH, D_FF), 0.5),
        'conv_b': nrm(ks[18], (DEPTH, D_FF), 0.02),
        'w_down': nrm(ks[19], (DEPTH, D_FF, D_MODEL), D_FF ** -0.5),
        'g_post_ffn': gain(ks[20]),
    }


def reference(x_prompt, x_sample, cache_k, cache_v, cache_idx_k, state_pool, state_conv, page_table,
              g_pre_mix, w_in, w_pool, pool_scale, w_out, g_post_mix, g_pre_ffn,
              w_gate, w_val, conv_w, conv_b, w_down, g_post_ffn):
    Bp, Sp = x_prompt.shape[:2]
    Ts = x_sample.shape[1]
    past = page_table.shape[1] * PAGE_SIZE
    pos_p = jnp.arange(Sp)
    pos_s = past + jnp.arange(Ts)
    yp, ys = x_prompt, x_sample
    kp_l, vp_l, ikp_l, plp_l, cvp_l = [], [], [], [], []
    ks_l, vs_l, iks_l, pls_l, cvs_l = [], [], [], [], []
    for l in range(DEPTH):
        p = dict(g_pre_mix=g_pre_mix[l], w_in=w_in[l], w_pool=w_pool[l], pool_scale=pool_scale[l],
                 w_out=w_out[l], g_post_mix=g_post_mix[l], g_pre_ffn=g_pre_ffn[l], w_gate=w_gate[l],
                 w_val=w_val[l], conv_w=conv_w[l], conv_b=conv_b[l], w_down=w_down[l],
                 g_post_ffn=g_post_ffn[l])
        pool0 = jnp.zeros((Bp, 0, POOL_WIDTH), yp.dtype)
        conv0 = jnp.zeros((Bp, CONV_WIDTH - 1, D_FF), yp.dtype)
        yp, k1, v1, ik1, pl1, cv1 = layer(yp, pos_p, prompt_attention, pool0, conv0, p)
        kp_l.append(k1); vp_l.append(v1); ikp_l.append(ik1); plp_l.append(pl1); cvp_l.append(cv1)
        ck, cv, cki = cache_k[l], cache_v[l], cache_idx_k[l]
        attn_s = lambda q, k, v, qi, ki, wi, ck=ck, cv=cv, cki=cki: sample_attention(
            q, k, v, qi, ki, wi, ck, cv, cki, page_table)
        ys, k2, v2, ik2, pl2, cv2 = layer(ys, pos_s, attn_s, state_pool[l], state_conv[l], p)
        ks_l.append(k2); vs_l.append(v2); iks_l.append(ik2); pls_l.append(pl2); cvs_l.append(cv2)
    return (yp, ys,
            jnp.stack(kp_l), jnp.stack(vp_l), jnp.stack(ikp_l), jnp.stack(plp_l), jnp.stack(cvp_l),
            jnp.stack(ks_l), jnp.stack(vs_l), jnp.stack(iks_l), jnp.stack(pls_l), jnp.stack(cvs_l))
```

```python
import functools

import jax
import jax.numpy as jnp
import numpy as np
from jax import lax
from jax.experimental import pallas as pl
from jax.experimental.pallas import tpu as pltpu

F32 = jnp.float32
BF16 = jnp.bfloat16
I32 = jnp.int32

N_HEADS = 8
N_KV_HEADS = 2
KV_GROUP = N_HEADS // N_KV_HEADS
HEAD_DIM = 64
N_IDX_HEADS = 8
IDX_DIM = 64
ROPE_THETA = 500000.0
ROT_HALF = HEAD_DIM // 8
TOPK_MAX = 256
PAGE_SIZE = 128
POOL_WINDOWS = (2, 4, 8, 16)
POOL_GROUP_DIM = 128
POOL_BUF = max(POOL_WINDOWS) - 1
CONV_WIDTH = 3
RMS_EPS = 1e-6

LANES = 128
SUBLANES = 8
KEY_BLOCK = 128
VMEM_LIMIT = 56 * 1024 * 1024

INT_MIN = np.int32(-2147483648)
NEG_BIG = -1e30


def _const_spec(shape):
    nd = len(shape)
    return pl.BlockSpec(shape, lambda *_: (0,) * nd, pipeline_mode=pl.Buffered(1))


def _rms(x, g):
    ms = jnp.mean(x * x, axis=-1, keepdims=True)
    return x * lax.rsqrt(ms + RMS_EPS) * g


def _sortable_key(s):
    bits = lax.bitcast_convert_type(s, I32)
    return bits ^ ((bits >> 31) & np.int32(0x7FFFFFFF))


def _rope_tables(pos):
    inv = ROPE_THETA ** (-jnp.arange(ROT_HALF, dtype=F32) / ROT_HALF)
    ang = pos.astype(F32)[:, None] * inv[None, :]
    cos, sin = jnp.cos(ang), jnp.sin(ang)
    p = pos.shape[0]
    one = jnp.ones((p, HEAD_DIM - 2 * ROT_HALF), F32)
    zero = jnp.zeros((p, HEAD_DIM - 2 * ROT_HALF), F32)
    z8 = jnp.zeros((p, ROT_HALF), F32)
    c64 = jnp.concatenate([cos, cos, one], axis=1)
    sa64 = jnp.concatenate([-sin, z8, zero], axis=1)
    sb64 = jnp.concatenate([z8, sin, zero], axis=1)
    rep = LANES // HEAD_DIM
    return dict(c=jnp.tile(c64, (1, rep)), sa=jnp.tile(sa64, (1, rep)), sb=jnp.tile(sb64, (1, rep)),
                cos_t=cos.T, sin_t=sin.T)


def _rope_rows(zc, c, sa, sb):
    return zc * c + pltpu.roll(zc, LANES - ROT_HALF, 1) * sa + pltpu.roll(zc, ROT_HALF, 1) * sb


def _rope_cols(zt, n_heads, cos_t, sin_t):
    pieces = []
    for h in range(n_heads):
        b = h * HEAD_DIM
        x1 = zt[b:b + ROT_HALF, :]
        x2 = zt[b + ROT_HALF:b + 2 * ROT_HALF, :]
        pieces.append(x1 * cos_t - x2 * sin_t)
        pieces.append(x2 * cos_t + x1 * sin_t)
        pieces.append(zt[b + 2 * ROT_HALF:b + HEAD_DIM, :])
    return jnp.concatenate(pieces, axis=0)


ROW_COLS = 128 + 128 + 128 + 512
T_ROWS = 512 + 512 + 128 + 16


def _inproj_prompt_kernel(x_ref, g_ref, wrow_ref, wt_ref, c_ref, sa_ref, sb_ref, cos_ref, sin_ref,
                          k_ref, v_ref, ki_ref, xp_ref, kb_ref, kib_ref, qt_ref, qit_ref, vt_ref, wit_ref,
                          *, tm):
    h = _rms(x_ref[...], g_ref[...]).astype(BF16)
    z = jnp.dot(h, wrow_ref[...], preferred_element_type=F32)
    c, sa, sb = c_ref[...], sa_ref[...], sb_ref[...]
    k = _rope_rows(z[:, 0:128], c, sa, sb)
    v = z[:, 128:256]
    ki = _rope_rows(z[:, 256:384], c, sa, sb)[:, :IDX_DIM]
    k_ref[...] = k
    v_ref[...] = v
    ki_ref[...] = ki
    xp_ref[...] = z[:, 384:896]
    kb_ref[...] = k.astype(BF16)
    kib_ref[...] = ki.astype(BF16)

    zt = lax.dot_general(wt_ref[...], h, (((1,), (1,)), ((), ())), preferred_element_type=F32)
    cos_t, sin_t = cos_ref[...], sin_ref[...]
    qt = _rope_cols(zt[0:512, :], N_HEADS, cos_t, sin_t) * (HEAD_DIM ** -0.5)
    qt_ref[...] = qt.astype(BF16)
    qit_ref[...] = _rope_cols(zt[512:1024, :], N_IDX_HEADS, cos_t, sin_t).astype(BF16)
    vt = zt[1024:1152, :].astype(BF16)
    for j in range(tm // KEY_BLOCK):
        vt_ref[j] = vt[:, j * KEY_BLOCK:(j + 1) * KEY_BLOCK]
    wit_ref[...] = zt[1152:1160, :] * ((IDX_DIM ** -0.5) * (N_IDX_HEADS ** -0.5))


def _inproj_prompt(x2d, g, wrow, wt, tabs, seq):
    n, d = x2d.shape
    tm = min(512, seq)
    nt = seq // tm
    grid = (n // tm,)
    row = lambda w: pl.BlockSpec((tm, w), lambda i: (i, 0))
    tab = pl.BlockSpec((tm, LANES), lambda i: (i % nt, 0))
    tab_t = pl.BlockSpec((ROT_HALF, tm), lambda i: (0, i % nt))
    col = lambda r: pl.BlockSpec((r, tm), lambda i: (0, i))
    out_shape = (
        jax.ShapeDtypeStruct((n, 128), F32), jax.ShapeDtypeStruct((n, 128), F32),
        jax.ShapeDtypeStruct((n, IDX_DIM), F32), jax.ShapeDtypeStruct((n, 512), F32),
        jax.ShapeDtypeStruct((n, 128), BF16), jax.ShapeDtypeStruct((n, IDX_DIM), BF16),
        jax.ShapeDtypeStruct((512, n), BF16), jax.ShapeDtypeStruct((512, n), BF16),
        jax.ShapeDtypeStruct((n // KEY_BLOCK, 128, KEY_BLOCK), BF16),
        jax.ShapeDtypeStruct((N_IDX_HEADS, n), F32),
    )
    out_specs = (row(128), row(128), row(IDX_DIM), row(512), row(128), row(IDX_DIM),
                 col(512), col(512),
                 pl.BlockSpec((tm // KEY_BLOCK, 128, KEY_BLOCK), lambda i: (i, 0, 0)),
                 col(N_IDX_HEADS))
    return pl.pallas_call(
        functools.partial(_inproj_prompt_kernel, tm=tm),
        grid=grid,
        in_specs=[row(d), _const_spec((1, d)), _const_spec(wrow.shape), _const_spec(wt.shape),
                  tab, tab, tab, tab_t, tab_t],
        out_specs=out_specs, out_shape=out_shape,
        compiler_params=pltpu.CompilerParams(dimension_semantics=("arbitrary",), vmem_limit_bytes=VMEM_LIMIT),
        name="inproj_prompt",
    )(x2d, g, wrow, wt, tabs["c"], tabs["sa"], tabs["sb"], tabs["cos_t"], tabs["sin_t"])


def _count_rows(hit):
    acc = hit[0:SUBLANES, :]
    for r in range(1, KEY_BLOCK // SUBLANES):
        acc = acc + hit[r * SUBLANES:(r + 1) * SUBLANES, :]
    return acc


def _attn_prompt_kernel(qt_ref, qit_ref, wit_ref, kb_ref, vt_ref, kib_ref, out_ref,
                        keys_ref, qbd_ref, qia_ref, m_ref, l_ref, acc_ref, *, topk, idx_bits):
    i = pl.program_id(1)
    nblk = i + 1

    zero_slab = jnp.zeros((HEAD_DIM, KEY_BLOCK), BF16)
    for h in range(N_HEADS):
        n = h // KV_GROUP
        cols = slice(h * KEY_BLOCK, (h + 1) * KEY_BLOCK)
        qbd_ref[n * HEAD_DIM:(n + 1) * HEAD_DIM, cols] = qt_ref[h * HEAD_DIM:(h + 1) * HEAD_DIM, :]
        qbd_ref[(1 - n) * HEAD_DIM:(2 - n) * HEAD_DIM, cols] = zero_slab
    for h in range(N_IDX_HEADS):
        qia_ref[:, h * KEY_BLOCK:(h + 1) * KEY_BLOCK] = qit_ref[h * IDX_DIM:(h + 1) * IDX_DIM, :]

    w = wit_ref[...]
    row_iota = lax.broadcasted_iota(I32, (KEY_BLOCK, KEY_BLOCK), 0)
    q_pos = i * KEY_BLOCK + lax.broadcasted_iota(I32, (KEY_BLOCK, KEY_BLOCK), 1)

    def score_body(kb, carry):
        off = pl.multiple_of(kb * KEY_BLOCK, KEY_BLOCK)
        dots = jnp.dot(kib_ref[pl.ds(off, KEY_BLOCK), :], qia_ref[...], preferred_element_type=F32)
        s = jnp.maximum(dots[:, 0:KEY_BLOCK], 0.0) * w[0:1, :]
        for h in range(1, N_IDX_HEADS):
            s = s + jnp.maximum(dots[:, h * KEY_BLOCK:(h + 1) * KEY_BLOCK], 0.0) * w[h:h + 1, :]
        key = jnp.where(off + row_iota <= q_pos, _sortable_key(s), INT_MIN)
        keys_ref[pl.ds(off, KEY_BLOCK), :] = key
        return carry

    lax.fori_loop(0, nblk, score_body, 0)

    def count_where(pred):
        def body(kb, acc):
            off = pl.multiple_of(kb * KEY_BLOCK, KEY_BLOCK)
            return acc + _count_rows(pred(keys_ref[pl.ds(off, KEY_BLOCK), :], off))
        acc = lax.fori_loop(0, nblk, body, jnp.zeros((SUBLANES, KEY_BLOCK), I32))
        return jnp.sum(acc.astype(F32), axis=0, keepdims=True)

    def bit_body(t, thr):
        cand = thr + (jnp.int32(1) << (31 - t))
        cnt = count_where(lambda blk, off: (blk >= cand).astype(I32))
        return jnp.where(cnt >= topk, cand, thr)

    thr = lax.fori_loop(0, 32, bit_body, jnp.full((1, KEY_BLOCK), INT_MIN, I32))
    thr = jnp.maximum(thr, INT_MIN + 1)
    cnt_ge = count_where(lambda blk, off: (blk >= thr).astype(I32))

    @pl.when(jnp.max(cnt_ge) > topk)
    def _():
        cnt_gt = count_where(lambda blk, off: (blk > thr).astype(I32))
        need = topk - cnt_gt

        def idx_body(t, x):
            cand = x + (jnp.int32(1) << (idx_bits - 1 - t))
            cnt = count_where(
                lambda blk, off: jnp.where(blk == thr, (off + row_iota < cand).astype(I32), 0))
            return jnp.where(cnt <= need - 1, cand, x)

        x = lax.fori_loop(0, idx_bits, idx_body, jnp.zeros((1, KEY_BLOCK), I32))

        def drop_body(kb, carry):
            off = pl.multiple_of(kb * KEY_BLOCK, KEY_BLOCK)
            blk = keys_ref[pl.ds(off, KEY_BLOCK), :]
            keys_ref[pl.ds(off, KEY_BLOCK), :] = jnp.where(
                blk == thr, jnp.where(off + row_iota > x, INT_MIN, blk), blk)
            return carry

        lax.fori_loop(0, nblk, drop_body, 0)

    m_ref[...] = jnp.full(m_ref.shape, NEG_BIG, F32)
    l_ref[...] = jnp.zeros(l_ref.shape, F32)
    acc_ref[...] = jnp.zeros(acc_ref.shape, F32)

    def attn_body(kb, carry):
        off = pl.multiple_of(kb * KEY_BLOCK, KEY_BLOCK)
        st = jnp.dot(kb_ref[pl.ds(off, KEY_BLOCK), :], qbd_ref[...], preferred_element_type=F32)
        sel = keys_ref[pl.ds(off, KEY_BLOCK), :] >= thr
        vt_blk = vt_ref[kb]
        m_old = m_ref[...]
        l_old = l_ref[...]
        m_new, l_new = [], []
        for n in range(N_KV_HEADS):
            p_cols, alphas = [], []
            for g in range(KV_GROUP):
                c = n * KV_GROUP + g
                cols = slice(c * KEY_BLOCK, (c + 1) * KEY_BLOCK)
                s_c = jnp.where(sel, st[:, cols], NEG_BIG)
                m_c = jnp.maximum(m_old[:, cols], jnp.max(s_c, axis=0, keepdims=True))
                alpha = jnp.exp(m_old[:, cols] - m_c)
                p_c = jnp.exp(s_c - m_c)
                l_new.append(alpha * l_old[:, cols] + jnp.sum(p_c, axis=0, keepdims=True))
                m_new.append(m_c)
                p_cols.append(p_c.astype(BF16))
                alphas.append(alpha)
            p_n = jnp.concatenate(p_cols, axis=1)
            a_n = jnp.concatenate(alphas, axis=1)
            rows = slice(n * HEAD_DIM, (n + 1) * HEAD_DIM)
            pv = jnp.dot(vt_blk[rows, :], p_n, preferred_element_type=F32)
            acc_ref[rows, :] = a_n * acc_ref[rows, :] + pv
        m_ref[...] = jnp.concatenate(m_new, axis=1)
        l_ref[...] = jnp.concatenate(l_new, axis=1)
        return carry

    lax.fori_loop(0, nblk, attn_body, 0)

    l_fin = l_ref[...]
    pieces = []
    for h in range(N_HEADS):
        n, g = divmod(h, KV_GROUP)
        cols = slice(g * KEY_BLOCK, (g + 1) * KEY_BLOCK)
        o = acc_ref[n * HEAD_DIM:(n + 1) * HEAD_DIM, cols] / l_fin[:, h * KEY_BLOCK:(h + 1) * KEY_BLOCK]
        pieces.append(o)
    out_ref[...] = jnp.transpose(jnp.concatenate(pieces, axis=0)).astype(out_ref.dtype)


def _attn_prompt(qt, qit, wit, kb, vt, kib, batch, seq):
    n = batch * seq
    nq = seq // KEY_BLOCK
    topk = min(TOPK_MAX, seq // 4)
    idx_bits = max(1, int(np.ceil(np.log2(seq))))
    qspec = lambda r: pl.BlockSpec((r, KEY_BLOCK), lambda b, i: (0, b * nq + i))
    kernel = functools.partial(_attn_prompt_kernel, topk=topk, idx_bits=idx_bits)
    return pl.pallas_call(
        kernel,
        grid=(batch, nq),
        in_specs=[qspec(512), qspec(512), qspec(N_IDX_HEADS),
                  pl.BlockSpec((seq, 128), lambda b, i: (b, 0)),
                  pl.BlockSpec((nq, 128, KEY_BLOCK), lambda b, i: (b, 0, 0)),
                  pl.BlockSpec((seq, IDX_DIM), lambda b, i: (b, 0))],
        out_specs=pl.BlockSpec((KEY_BLOCK, 512), lambda b, i: (b * nq + i, 0)),
        out_shape=jax.ShapeDtypeStruct((n, 512), BF16),
        scratch_shapes=[pltpu.VMEM((seq, KEY_BLOCK), I32),
                        pltpu.VMEM((128, N_HEADS * KEY_BLOCK), BF16),
                        pltpu.VMEM((IDX_DIM, N_IDX_HEADS * KEY_BLOCK), BF16),
                        pltpu.VMEM((1, N_HEADS * KEY_BLOCK), F32),
                        pltpu.VMEM((1, N_HEADS * KEY_BLOCK), F32),
                        pltpu.VMEM((128, KV_GROUP * KEY_BLOCK), F32)],
        compiler_params=pltpu.CompilerParams(dimension_semantics=("arbitrary", "arbitrary"),
                                             vmem_limit_bytes=VMEM_LIMIT),
        name="attn_prompt",
    )(qt, qit, wit, kb, vt, kib)


def _mix_tail(a_bf, diff, x, wpool_ref, pscale_ref, wout_ref, gpost_ref, gpre_ref, x1_ref, h2_ref):
    outs = []
    for g in range(len(POOL_WINDOWS)):
        cols = slice(g * POOL_GROUP_DIM, (g + 1) * POOL_GROUP_DIM)
        outs.append(jnp.dot(diff[:, cols].astype(BF16), wpool_ref[g], preferred_element_type=F32))
    m = jnp.concatenate(outs, axis=1) * pscale_ref[...]
    mix_in = jnp.concatenate([a_bf, m.astype(BF16)], axis=1)
    mix = jnp.dot(mix_in, wout_ref[...], preferred_element_type=F32)
    x1 = x + _rms(mix, gpost_ref[...])
    x1_ref[...] = x1
    h2_ref[...] = _rms(x1, gpre_ref[...]).astype(BF16)


def _mix_prompt_kernel(a_ref, xp_ref, x_ref, wpool_ref, pscale_ref, wout_ref, gpost_ref, gpre_ref,
                       x1_ref, h2_ref, prev_ref, *, tm, nt):
    i = pl.program_id(0)
    halo = POOL_BUF + 1

    @pl.when(i % nt == 0)
    def _():
        prev_ref[...] = jnp.zeros(prev_ref.shape, F32)

    xp = xp_ref[...]
    ext = jnp.concatenate([prev_ref[...], xp], axis=0)
    prev_ref[...] = xp[tm - halo:, :]
    t = (i % nt) * tm + lax.broadcasted_iota(I32, (tm, 1), 0)
    diffs = []
    for g, win in enumerate(POOL_WINDOWS):
        cols = slice(g * POOL_GROUP_DIM, (g + 1) * POOL_GROUP_DIM)
        s = ext[:, cols]
        sh = 1
        while sh < win:
            s = s + pltpu.roll(s, sh, 0)
            sh *= 2
        cnt = jnp.minimum(t + 1, win).astype(F32)
        diffs.append(s[halo:, :] / cnt - xp[:, cols])
    diff = jnp.concatenate(diffs, axis=1)
    _mix_tail(a_ref[...], diff, x_ref[...], wpool_ref, pscale_ref, wout_ref, gpost_ref, gpre_ref,
              x1_ref, h2_ref)


def _mix_prompt(a, xp, x2d, wpool, pscale, wout, gpost, gpre, seq):
    n, d = x2d.shape
    tm = min(512, seq)
    nt = seq // tm
    row = lambda w: pl.BlockSpec((tm, w), lambda i: (i, 0))
    return pl.pallas_call(
        functools.partial(_mix_prompt_kernel, tm=tm, nt=nt),
        grid=(n // tm,),
        in_specs=[row(512), row(512), row(d), _const_spec(wpool.shape), _const_spec((1, 512)),
                  _const_spec(wout.shape), _const_spec((1, d)), _const_spec((1, d))],
        out_specs=(row(d), row(d)),
        out_shape=(jax.ShapeDtypeStruct((n, d), F32), jax.ShapeDtypeStruct((n, d), BF16)),
        scratch_shapes=[pltpu.VMEM((POOL_BUF + 1, 512), F32)],
        compiler_params=pltpu.CompilerParams(dimension_semantics=("arbitrary",), vmem_limit_bytes=VMEM_LIMIT),
        name="mix_prompt",
    )(a, xp, x2d, wpool, pscale, wout, gpost, gpre)


def _mix_sample_kernel(a_ref, ext_ref, x_ref, wpool_ref, pscale_ref, wout_ref, gpost_ref, gpre_ref,
                       x1_ref, h2_ref):
    rows = POOL_BUF + 1
    last = ext_ref[rows - 1]
    diffs = []
    for g, win in enumerate(POOL_WINDOWS):
        cols = slice(g * POOL_GROUP_DIM, (g + 1) * POOL_GROUP_DIM)
        wsum = last[:, cols]
        for r in range(rows - win, rows - 1):
            wsum = wsum + ext_ref[r][:, cols]
        diffs.append(wsum / float(win) - last[:, cols])
    diff = jnp.concatenate(diffs, axis=1)
    _mix_tail(a_ref[...].astype(BF16), diff, x_ref[...], wpool_ref, pscale_ref, wout_ref, gpost_ref,
              gpre_ref, x1_ref, h2_ref)


def _mix_sample(a, ext, x2d, wpool, pscale, wout, gpost, gpre):
    n, d = x2d.shape
    full = lambda s: pl.BlockSpec(s, lambda: (0,) * len(s))
    return pl.pallas_call(
        _mix_sample_kernel,
        in_specs=[full(a.shape), full(ext.shape), full(x2d.shape), full(wpool.shape), full((1, 512)),
                  full(wout.shape), full((1, d)), full((1, d))],
        out_specs=(full((n, d)), full((n, d))),
        out_shape=(jax.ShapeDtypeStruct((n, d), F32), jax.ShapeDtypeStruct((n, d), BF16)),
        compiler_params=pltpu.CompilerParams(vmem_limit_bytes=VMEM_LIMIT),
        name="mix_sample",
    )(a, ext, x2d, wpool, pscale, wout, gpost, gpre)


def _ffn_tail(c, val, x1, wdown_ref, gpost_ref, x2_ref):
    y = (jax.nn.gelu(c, approximate=True) * val).astype(BF16)
    f = jnp.dot(y, wdown_ref[...], preferred_element_type=F32)
    x2_ref[...] = x1 + _rms(f, gpost_ref[...])


def _ffn_prompt_kernel(h2_ref, x1_ref, wg_ref, wv_ref, cw_ref, wdown_ref, gpost_ref,
                       x2_ref, cst_ref, prev_ref, *, tm, nt):
    i = pl.program_id(0)

    @pl.when(i % nt == 0)
    def _():
        prev_ref[...] = jnp.zeros(prev_ref.shape, F32)

    h2 = h2_ref[...]
    g = jnp.dot(h2, wg_ref[...], preferred_element_type=F32)
    val = jnp.dot(h2, wv_ref[...], preferred_element_type=F32)
    ext = jnp.concatenate([prev_ref[...], g], axis=0)
    tail = g[tm - SUBLANES:, :]
    prev_ref[...] = tail
    cst_ref[0] = tail
    cw = cw_ref[...]
    g1 = pltpu.roll(ext, 1, 0)[SUBLANES:, :]
    g2 = pltpu.roll(ext, 2, 0)[SUBLANES:, :]
    c = cw[3:4, :] + g2 * cw[0:1, :] + g1 * cw[1:2, :] + g * cw[2:3, :]
    _ffn_tail(c, val, x1_ref[...], wdown_ref, gpost_ref, x2_ref)


def _ffn_prompt(h2, x1, wg, wv, cw, wdown, gpost, batch, seq):
    n, d = x1.shape
    f = wg.shape[1]
    tm = min(256, seq)
    nt = seq // tm
    row = lambda w: pl.BlockSpec((tm, w), lambda i: (i, 0))
    return pl.pallas_call(
        functools.partial(_ffn_prompt_kernel, tm=tm, nt=nt),
        grid=(n // tm,),
        in_specs=[row(d), row(d), _const_spec(wg.shape), _const_spec(wv.shape), _const_spec(cw.shape),
                  _const_spec(wdown.shape), _const_spec((1, d))],
        out_specs=(row(d), pl.BlockSpec((1, SUBLANES, f), lambda i: (i // nt, 0, 0))),
        out_shape=(jax.ShapeDtypeStruct((n, d), F32), jax.ShapeDtypeStruct((batch, SUBLANES, f), F32)),
        scratch_shapes=[pltpu.VMEM((SUBLANES, f), F32)],
        compiler_params=pltpu.CompilerParams(dimension_semantics=("arbitrary",), vmem_limit_bytes=VMEM_LIMIT),
        name="ffn_prompt",
    )(h2, x1, wg, wv, cw, wdown, gpost)


def _ffn_sample_kernel(h2_ref, x1_ref, b0_ref, b1_ref, wg_ref, wv_ref, cw_ref, wdown_ref, gpost_ref,
                       x2_ref, g_ref):
    h2 = h2_ref[...]
    g = jnp.dot(h2, wg_ref[...], preferred_element_type=F32)
    val = jnp.dot(h2, wv_ref[...], preferred_element_type=F32)
    g_ref[...] = g
    cw = cw_ref[...]
    c = cw[3:4, :] + b0_ref[...] * cw[0:1, :] + b1_ref[...] * cw[1:2, :] + g * cw[2:3, :]
    _ffn_tail(c, val, x1_ref[...], wdown_ref, gpost_ref, x2_ref)


def _ffn_sample(h2, x1, b0, b1, wg, wv, cw, wdown, gpost):
    n, d = x1.shape
    f = wg.shape[1]
    full = lambda s: pl.BlockSpec(s, lambda: (0,) * len(s))
    args = (h2, x1, b0, b1, wg, wv, cw, wdown, gpost)
    return pl.pallas_call(
        _ffn_sample_kernel,
        in_specs=[full(a.shape) for a in args],
        out_specs=(full((n, d)), full((n, f))),
        out_shape=(jax.ShapeDtypeStruct((n, d), F32), jax.ShapeDtypeStruct((n, f), F32)),
        compiler_params=pltpu.CompilerParams(vmem_limit_bytes=VMEM_LIMIT),
        name="ffn_sample",
    )(*args)


def _inproj_sample_kernel(x_ref, g_ref, w_ref, c_ref, sa_ref, sb_ref,
                          q_ref, k_ref, v_ref, qi_ref, kiw_ref, xp_ref):
    h = _rms(x_ref[...], g_ref[...]).astype(BF16)
    z = jnp.dot(h, w_ref[...], preferred_element_type=F32)
    c, sa, sb = c_ref[...], sa_ref[...], sb_ref[...]
    rope = lambda lo: _rope_rows(z[:, lo:lo + LANES], c, sa, sb)
    q_ref[...] = jnp.concatenate([rope(j * LANES) for j in range(4)], axis=1) * (HEAD_DIM ** -0.5)
    k_ref[...] = rope(512)
    v_ref[...] = z[:, 640:768]
    qi_ref[...] = jnp.concatenate([rope(768 + j * LANES) for j in range(4)], axis=1)
    lane = lax.broadcasted_iota(I32, (1, LANES), 1)
    kiw = z[:, 1280:1408]
    kiw_ref[...] = jnp.where(lane < IDX_DIM, _rope_rows(kiw, c, sa, sb), kiw)
    xp_ref[...] = z[:, 1408:1920]


def _inproj_sample(x2d, g, wall, tabs):
    n, d = x2d.shape
    full = lambda s: pl.BlockSpec(s, lambda: (0,) * len(s))
    args = (x2d, g, wall, tabs["c"], tabs["sa"], tabs["sb"])
    widths = (512, 128, 128, 512, 128, 512)
    return pl.pallas_call(
        _inproj_sample_kernel,
        in_specs=[full(a.shape) for a in args],
        out_specs=tuple(full((n, w)) for w in widths),
        out_shape=tuple(jax.ShapeDtypeStruct((n, w), F32) for w in widths),
        compiler_params=pltpu.CompilerParams(vmem_limit_bytes=VMEM_LIMIT),
        name="inproj_sample",
    )(*args)


PAGE_ROWS = PAGE_SIZE // 2


def _page_copy(cache_ref, buf_ref, sem_ref, page, slot, j):
    return pltpu.make_async_copy(cache_ref.at[page], buf_ref.at[slot, pl.ds(j * PAGE_ROWS, PAGE_ROWS)],
                                 sem_ref.at[slot])


def _chunk_dma(pt_ref, caches, bufs, sems, b, c, slot, cp, start):
    for j in range(cp):
        page = pt_ref[b, c * cp + j]
        for cache_ref, buf_ref, sem_ref in zip(caches, bufs, sems):
            cp_desc = _page_copy(cache_ref, buf_ref, sem_ref, page, slot, j)
            if start:
                cp_desc.start()
            else:
                cp_desc.wait()


def _pipeline_step(pt_ref, caches, bufs, sems, cp):
    b, c = pl.program_id(0), pl.program_id(1)
    nb, nc = pl.num_programs(0), pl.num_programs(1)
    step = b * nc + c
    slot = step % 2

    @pl.when(step == 0)
    def _():
        _chunk_dma(pt_ref, caches, bufs, sems, b, c, slot, cp, True)

    @pl.when(step + 1 < nb * nc)
    def _():
        nxt = step + 1
        _chunk_dma(pt_ref, caches, bufs, sems, nxt // nc, nxt % nc, 1 - slot, cp, True)

    _chunk_dma(pt_ref, caches, bufs, sems, b, c, slot, cp, False)
    return slot


def _decode_select_kernel(pt_ref, qi_ref, w_ref, kin_ref, cache_ref, bias_ref, bnew_ref,
                          buf_ref, sem_ref, keys_ref, *, cp, topk, idx_bits, past):
    c = pl.program_id(1)
    nc = pl.num_programs(1)
    slot = _pipeline_step(pt_ref, (cache_ref,), (buf_ref,), (sem_ref,), cp)
    cpr = cp * PAGE_ROWS

    qi = qi_ref[0]
    w16 = w_ref[0][:, 0:1]
    chunk = buf_ref[slot].astype(BF16)
    dots = lax.dot_general(qi, chunk, (((1,), (1,)), ((), ())), preferred_element_type=F32)
    s16 = jnp.maximum(dots, 0.0) * w16
    for par in range(2):
        s = jnp.sum(s16[par * N_IDX_HEADS:(par + 1) * N_IDX_HEADS, :], axis=0, keepdims=True)
        keys_ref[par, pl.ds(c, 1), :] = _sortable_key(s)

    @pl.when(c == nc - 1)
    def _():
        qi_f = qi[0:N_IDX_HEADS, 0:IDX_DIM].astype(F32)
        ki_new = kin_ref[0][:, 0:IDX_DIM]
        d_new = jnp.sum(qi_f * ki_new.astype(BF16).astype(F32), axis=1, keepdims=True)
        s_new = jnp.sum(jnp.maximum(d_new, 0.0) * w16[0:N_IDX_HEADS, :], axis=0, keepdims=True)
        key_new = _sortable_key(s_new)
        k_even, k_odd = keys_ref[0], keys_ref[1]

        one = lambda pred: jnp.where(pred, 1.0, 0.0)

        def total(fe, fo, fn):
            return jnp.sum(fe, keepdims=True) + jnp.sum(fo, keepdims=True) + fn

        def bit_body(t, thr):
            cand = thr + (jnp.int32(1) << (31 - t))
            cnt = total(one(k_even >= cand), one(k_odd >= cand), one(key_new >= cand))
            return jnp.where(cnt >= topk, cand, thr)

        thr = lax.fori_loop(0, 32, bit_body, jnp.full((1, 1), INT_MIN, I32))
        thr = jnp.maximum(thr, INT_MIN + 1)
        need = topk - total(one(k_even > thr), one(k_odd > thr), one(key_new > thr))

        base = (lax.broadcasted_iota(I32, k_even.shape, 0) * (cp * PAGE_SIZE)
                + 2 * lax.broadcasted_iota(I32, k_even.shape, 1))
        idx_e, idx_o = base, base + 1
        tie_e, tie_o, tie_n = one(k_even == thr), one(k_odd == thr), one(key_new == thr)

        def idx_body(t, x):
            cand = x + (jnp.int32(1) << (idx_bits - 1 - t))
            cnt = total(tie_e * one(idx_e < cand), tie_o * one(idx_o < cand), tie_n * one(past < cand))
            return jnp.where(cnt <= need - 1.0, cand, x)

        x = lax.fori_loop(0, idx_bits, idx_body, jnp.zeros((1, 1), I32))
        sel_e = jnp.where(k_even == thr, one(idx_e <= x), one(k_even > thr))
        sel_o = jnp.where(k_odd == thr, one(idx_o <= x), one(k_odd > thr))
        sel_n = jnp.where(key_new == thr, one(past <= x), one(key_new > thr))
        bias_ref[0, 0] = (1.0 - sel_e) * NEG_BIG
        bias_ref[0, 1] = (1.0 - sel_o) * NEG_BIG
        bnew_ref[0] = jnp.broadcast_to((1.0 - sel_n) * NEG_BIG, (1, LANES))


def _decode_attend_kernel(pt_ref, q_ref, bias_ref, bnew_ref, kn_ref, vn_ref, ck_ref, cv_ref, out_ref,
                          kbuf_ref, vbuf_ref, ksem_ref, vsem_ref, m_ref, l_ref, acc_ref, *, cp):
    c = pl.program_id(1)
    nc = pl.num_programs(1)
    slot = _pipeline_step(pt_ref, (ck_ref, cv_ref), (kbuf_ref, vbuf_ref), (ksem_ref, vsem_ref), cp)
    q = q_ref[0]
    half = N_HEADS

    @pl.when(c == 0)
    def _():
        k_new = kn_ref[0].astype(BF16).astype(F32)
        s_new = jnp.sum(q[0:half, 0:LANES].astype(F32) * k_new, axis=1, keepdims=True) + bnew_ref[0][:, 0:1]
        m_ref[...] = jnp.concatenate([jnp.broadcast_to(s_new, (half, LANES)),
                                      jnp.full((half, LANES), NEG_BIG, F32)], axis=0)
        l_ref[...] = jnp.concatenate([jnp.ones((half, LANES), F32), jnp.zeros((half, LANES), F32)], axis=0)
        v_new = jnp.broadcast_to(vn_ref[0].astype(BF16).astype(F32), (half, LANES))
        acc_ref[...] = jnp.concatenate(
            [jnp.concatenate([v_new, jnp.zeros((half, LANES), F32)], axis=1),
             jnp.zeros((half, 2 * LANES), F32)], axis=0)

    kc = kbuf_ref[slot].astype(BF16)
    vc = vbuf_ref[slot].astype(BF16)
    st = lax.dot_general(q, kc, (((1,), (1,)), ((), ())), preferred_element_type=F32)
    cpr = st.shape[1]
    bias = jnp.concatenate([jnp.broadcast_to(bias_ref[0, 0, pl.ds(c, 1), :], (half, cpr)),
                            jnp.broadcast_to(bias_ref[0, 1, pl.ds(c, 1), :], (half, cpr))], axis=0)
    s = st + bias
    m_old = m_ref[...][:, 0:1]
    m_new = jnp.maximum(m_old, jnp.max(s, axis=1, keepdims=True))
    alpha = jnp.exp(m_old - m_new)
    p = jnp.exp(s - m_new)
    l_ref[...] = alpha * l_ref[...] + jnp.sum(p, axis=1, keepdims=True)
    acc_ref[...] = alpha * acc_ref[...] + jnp.dot(p.astype(BF16), vc, preferred_element_type=F32)
    m_ref[...] = jnp.broadcast_to(m_new, m_ref.shape)

    @pl.when(c == nc - 1)
    def _():
        m = m_ref[...][:, 0:1]
        m_fin = jnp.maximum(m[0:half], m[half:])
        e0, e1 = jnp.exp(m[0:half] - m_fin), jnp.exp(m[half:] - m_fin)
        l = l_ref[...][:, 0:1]
        acc = acc_ref[...]
        o = (acc[0:half, 0:LANES] * e0 + acc[half:, LANES:] * e1) / (l[0:half] * e0 + l[half:] * e1)
        head = lax.broadcasted_iota(I32, (half, HEAD_DIM), 0)
        out_ref[0] = jnp.where(head < KV_GROUP, o[:, 0:HEAD_DIM], o[:, HEAD_DIM:])


def _decode_attention(page_table, q, qi, kiw, k_new, v_new, cache_k, cache_v, cache_ik):
    nb, n_pages = page_table.shape
    past = n_pages * PAGE_SIZE
    topk = min(TOPK_MAX, (past + 1) // 4)
    idx_bits = int(np.floor(np.log2(past))) + 1
    cp = min(16, n_pages)
    nc = n_pages // cp
    cpr = cp * PAGE_ROWS
    n_phys = cache_k.shape[0]

    z = lambda *s: jnp.zeros(s, F32)
    qi3 = qi.reshape(nb, N_IDX_HEADS, IDX_DIM)
    qi_bd = jnp.concatenate([jnp.concatenate([qi3, z(nb, 8, 64)], axis=2),
                             jnp.concatenate([z(nb, 8, 64), qi3], axis=2)], axis=1).astype(BF16)
    q3 = q.reshape(nb, N_KV_HEADS, KV_GROUP, HEAD_DIM)
    qg = jnp.concatenate([jnp.concatenate([q3[:, 0], z(nb, 4, 64)], axis=2),
                          jnp.concatenate([z(nb, 4, 64), q3[:, 1]], axis=2)], axis=1)
    q_bd = jnp.concatenate([jnp.concatenate([qg, z(nb, 8, 128)], axis=2),
                            jnp.concatenate([z(nb, 8, 128), qg], axis=2)], axis=1).astype(BF16)
    wi = kiw[:, IDX_DIM:IDX_DIM + N_IDX_HEADS] * ((IDX_DIM ** -0.5) * (N_IDX_HEADS ** -0.5))
    w16 = jnp.broadcast_to(jnp.concatenate([wi, wi], axis=1)[:, :, None], (nb, 16, LANES))
    kin3, kn3, vn3 = kiw[:, None, :], k_new[:, None, :], v_new[:, None, :]
    ik_view = cache_ik.reshape(n_phys, PAGE_ROWS, 2 * IDX_DIM)
    ck_view = cache_k.reshape(n_phys, PAGE_ROWS, 2 * N_KV_HEADS * HEAD_DIM)
    cv_view = cache_v.reshape(n_phys, PAGE_ROWS, 2 * N_KV_HEADS * HEAD_DIM)

    seq_blk = lambda s: pl.BlockSpec((1,) + s, lambda b, c, pt: (b,) + (0,) * len(s))
    any_spec = pl.BlockSpec(memory_space=pl.ANY)
    params = pltpu.CompilerParams(dimension_semantics=("arbitrary", "arbitrary"), vmem_limit_bytes=VMEM_LIMIT)

    bias, bnew = pl.pallas_call(
        functools.partial(_decode_select_kernel, cp=cp, topk=topk, idx_bits=idx_bits, past=past),
        grid_spec=pltpu.PrefetchScalarGridSpec(
            num_scalar_prefetch=1, grid=(nb, nc),
            in_specs=[seq_blk((16, LANES)), seq_blk((16, LANES)), seq_blk((1, LANES)), any_spec],
            out_specs=(seq_blk((2, nc, cpr)), seq_blk((1, LANES))),
            scratch_shapes=[pltpu.VMEM((2, cpr, LANES), F32), pltpu.SemaphoreType.DMA((2,)),
                            pltpu.VMEM((2, nc, cpr), I32)]),
        out_shape=(jax.ShapeDtypeStruct((nb, 2, nc, cpr), F32), jax.ShapeDtypeStruct((nb, 1, LANES), F32)),
        compiler_params=params, name="decode_select",
    )(page_table, qi_bd, w16, kin3, ik_view)

    out = pl.pallas_call(
        functools.partial(_decode_attend_kernel, cp=cp),
        grid_spec=pltpu.PrefetchScalarGridSpec(
            num_scalar_prefetch=1, grid=(nb, nc),
            in_specs=[seq_blk((16, 2 * LANES)), seq_blk((2, nc, cpr)), seq_blk((1, LANES)),
                      seq_blk((1, LANES)), seq_blk((1, LANES)), any_spec, any_spec],
            out_specs=seq_blk((N_HEADS, HEAD_DIM)),
            scratch_shapes=[pltpu.VMEM((2, cpr, 2 * LANES), F32), pltpu.VMEM((2, cpr, 2 * LANES), F32),
                            pltpu.SemaphoreType.DMA((2,)), pltpu.SemaphoreType.DMA((2,)),
                            pltpu.VMEM((16, LANES), F32), pltpu.VMEM((16, LANES), F32),
                            pltpu.VMEM((16, 2 * LANES), F32)]),
        out_shape=jax.ShapeDtypeStruct((nb, N_HEADS, HEAD_DIM), F32),
        compiler_params=params, name="decode_attend",
    )(page_table, q_bd, bias, bnew, kn3, vn3, ck_view, cv_view)
    return out.reshape(nb, N_HEADS * HEAD_DIM)


def _prep_layer(w_in, w_pool, pool_scale, w_out, w_gate, w_val, conv_w, conv_b, w_down,
                g_pre_mix, g_post_mix, g_pre_ffn, g_post_ffn):
    d = w_in.shape[0]
    sizes = (512, 128, 128, 512, IDX_DIM, N_IDX_HEADS, 512)
    cuts = np.cumsum(sizes)[:-1]
    wq, wk, wv, wqi, wki, wwi, wxp = jnp.split(w_in, [int(c) for c in cuts], axis=1)
    pad = lambda n: jnp.zeros((d, n), w_in.dtype)
    f = w_gate.shape[1]
    cw = jnp.concatenate([conv_w, conv_b[None, :], jnp.zeros((SUBLANES - CONV_WIDTH - 1, f), F32)], axis=0)
    return dict(
        wrow=jnp.concatenate([wk, wv, wki, pad(LANES - IDX_DIM), wxp], axis=1).astype(BF16),
        wt=jnp.concatenate([wq, wqi, wv, wwi, pad(16 - N_IDX_HEADS)], axis=1).T.astype(BF16),
        wall=jnp.concatenate([wq, wk, wv, wqi, wki, wwi, pad(LANES - IDX_DIM - N_IDX_HEADS), wxp],
                             axis=1).astype(BF16),
        wpool=w_pool.astype(BF16), pscale=pool_scale[None, :], wout=w_out.astype(BF16),
        wg=w_gate.astype(BF16), wv=w_val.astype(BF16), cw=cw, wdown=w_down.astype(BF16),
        g_pre_mix=g_pre_mix[None, :], g_post_mix=g_post_mix[None, :],
        g_pre_ffn=g_pre_ffn[None, :], g_post_ffn=g_post_ffn[None, :])


def kernel(x_prompt, x_sample, cache_k, cache_v, cache_idx_k, state_pool, state_conv, page_table,
           g_pre_mix, w_in, w_pool, pool_scale, w_out, g_post_mix, g_pre_ffn,
           w_gate, w_val, conv_w, conv_b, w_down, g_post_ffn):
    bp, sp, d = x_prompt.shape
    bs, ts, _ = x_sample.shape
    assert ts == 1 and sp % KEY_BLOCK == 0 and sp >= 2 * KEY_BLOCK
    depth = w_in.shape[0]
    past = page_table.shape[1] * PAGE_SIZE
    f = w_gate.shape[2]
    tabs_p = _rope_tables(jnp.arange(sp))
    tabs_s = _rope_tables(jnp.full((bs,), past, I32))

    yp = x_prompt.reshape(bp * sp, d)
    ys = x_sample.reshape(bs, d)
    outs = [[] for _ in range(10)]
    for l in range(depth):
        p = _prep_layer(w_in[l], w_pool[l], pool_scale[l], w_out[l], w_gate[l], w_val[l], conv_w[l],
                        conv_b[l], w_down[l], g_pre_mix[l], g_post_mix[l], g_pre_ffn[l], g_post_ffn[l])
        k, v, ki, xp, kb, kib, qt, qit, vt, wit = _inproj_prompt(yp, p["g_pre_mix"], p["wrow"], p["wt"], tabs_p, sp)
        a = _attn_prompt(qt, qit, wit, kb, vt, kib, bp, sp)
        x1, h2 = _mix_prompt(a, xp, yp, p["wpool"], p["pscale"], p["wout"], p["g_post_mix"], p["g_pre_ffn"], sp)
        yp, cst = _ffn_prompt(h2, x1, p["wg"], p["wv"], p["cw"], p["wdown"], p["g_post_ffn"], bp, sp)
        outs[0].append(k.reshape(bp, sp, N_KV_HEADS, HEAD_DIM))
        outs[1].append(v.reshape(bp, sp, N_KV_HEADS, HEAD_DIM))
        outs[2].append(ki.reshape(bp, sp, IDX_DIM))
        outs[3].append(xp.reshape(bp, sp, 512)[:, sp - POOL_BUF:, :])
        outs[4].append(cst[:, SUBLANES - (CONV_WIDTH - 1):, :])
        q_s, k_s, v_s, qi_s, kiw_s, xp_s = _inproj_sample(ys, p["g_pre_mix"], p["wall"], tabs_s)
        a_s = _decode_attention(page_table, q_s, qi_s, kiw_s, k_s, v_s, cache_k[l], cache_v[l], cache_idx_k[l])
        ext = jnp.concatenate([state_pool[l], xp_s[:, None, :]], axis=1)
        x1_s, h2_s = _mix_sample(a_s, jnp.transpose(ext, (1, 0, 2)), ys, p["wpool"], p["pscale"], p["wout"], p["g_post_mix"], p["g_pre_ffn"])
        ys, g_s = _ffn_sample(h2_s, x1_s, state_conv[l, :, 0, :], state_conv[l, :, 1, :],
                              p["wg"], p["wv"], p["cw"], p["wdown"], p["g_post_ffn"])
        outs[5].append(k_s.reshape(bs, 1, N_KV_HEADS, HEAD_DIM))
        outs[6].append(v_s.reshape(bs, 1, N_KV_HEADS, HEAD_DIM))
        outs[7].append(kiw_s[:, None, :IDX_DIM])
        outs[8].append(ext[:, 1:, :])
        outs[9].append(jnp.concatenate([state_conv[l, :, 1:, :], g_s[:, None, :]], axis=1))
    return (yp.reshape(bp, sp, d), ys.reshape(bs, 1, d)) + tuple(jnp.stack(o) for o in outs)
```

```python
import functools

import jax
import jax.numpy as jnp
import numpy as np
from jax import lax
from jax.experimental import pallas as pl
from jax.experimental.pallas import tpu as pltpu

F32 = jnp.float32
BF16 = jnp.bfloat16
I32 = jnp.int32

N_HEADS = 8
N_KV_HEADS = 2
KV_GROUP = N_HEADS // N_KV_HEADS
HEAD_DIM = 64
N_IDX_HEADS = 8
IDX_DIM = 64
ROPE_THETA = 500000.0
ROT_HALF = HEAD_DIM // 8
TOPK_MAX = 256
PAGE_SIZE = 128
POOL_WINDOWS = (2, 4, 8, 16)
POOL_GROUP_DIM = 128
POOL_BUF = max(POOL_WINDOWS) - 1
CONV_WIDTH = 3
RMS_EPS = 1e-6

LANES = 128
SUBLANES = 8
BF16_ROWS = 16
KEY_BLOCK = 128
SEQ_TILE = 512
VMEM_LIMIT = 56 * 1024 * 1024
LOG2_E = 1.4426950408889634

INT_MIN = np.int32(-2147483648)
NEG_BIG = -1e30


def _const_spec(shape):
    nd = len(shape)
    return pl.BlockSpec(shape, lambda *_: (0,) * nd, pipeline_mode=pl.Buffered(1))


def _rms(x, g):
    ms = jnp.mean(x * x, axis=-1, keepdims=True)
    return x * lax.rsqrt(ms + RMS_EPS) * g


def _sortable_key(s):
    bits = lax.bitcast_convert_type(s, I32)
    return bits ^ ((bits >> 31) & np.int32(0x7FFFFFFF))


def _tree_sum(parts):
    parts = list(parts)
    while len(parts) > 1:
        nxt = [parts[k] + parts[k + 1] for k in range(0, len(parts) - 1, 2)]
        if len(parts) % 2:
            nxt.append(parts[-1])
        parts = nxt
    return parts[0]


def _rope_tables(pos):
    inv = ROPE_THETA ** (-jnp.arange(ROT_HALF, dtype=F32) / ROT_HALF)
    ang = pos.astype(F32)[:, None] * inv[None, :]
    cos, sin = jnp.cos(ang), jnp.sin(ang)
    p = pos.shape[0]
    one = jnp.ones((p, HEAD_DIM - 2 * ROT_HALF), F32)
    zero = jnp.zeros((p, HEAD_DIM - 2 * ROT_HALF), F32)
    z8 = jnp.zeros((p, ROT_HALF), F32)
    c64 = jnp.concatenate([cos, cos, one], axis=1)
    sa64 = jnp.concatenate([-sin, z8, zero], axis=1)
    sb64 = jnp.concatenate([z8, sin, zero], axis=1)
    rep = LANES // HEAD_DIM
    return dict(c=jnp.tile(c64, (1, rep)), sa=jnp.tile(sa64, (1, rep)), sb=jnp.tile(sb64, (1, rep)),
                cos_t=cos.T, sin_t=sin.T)


def _rope_rows(zc, c, sa, sb):
    return zc * c + pltpu.roll(zc, LANES - ROT_HALF, 1) * sa + pltpu.roll(zc, ROT_HALF, 1) * sb


def _rope_cols(zt, n_heads, cos_t, sin_t):
    pieces = []
    for h in range(n_heads):
        b = h * HEAD_DIM
        x1 = zt[b:b + ROT_HALF, :]
        x2 = zt[b + ROT_HALF:b + 2 * ROT_HALF, :]
        pieces.append(x1 * cos_t - x2 * sin_t)
        pieces.append(x2 * cos_t + x1 * sin_t)
        pieces.append(zt[b + 2 * ROT_HALF:b + HEAD_DIM, :])
    return jnp.concatenate(pieces, axis=0)


def _inproj_prompt_kernel(x_ref, g_ref, wrow_ref, wt_ref, c_ref, sa_ref, sb_ref, cos_ref, sin_ref,
                          k_ref, v_ref, ki_ref, xp_ref, kb_ref, kib_ref, qt_ref, qit_ref, vt_ref, wit_ref):
    h = _rms(x_ref[...], g_ref[...]).astype(BF16)
    z = jnp.dot(h, wrow_ref[...], preferred_element_type=F32)
    c, sa, sb = c_ref[...], sa_ref[...], sb_ref[...]
    k = _rope_rows(z[:, 0:128], c, sa, sb)
    v = z[:, 128:256]
    ki = _rope_rows(z[:, 256:384], c, sa, sb)[:, :IDX_DIM]
    k_ref[...] = k
    v_ref[...] = v
    ki_ref[...] = ki
    xp_ref[...] = z[:, 384:896]
    kb_ref[...] = k.astype(BF16)
    kib_ref[...] = ki.astype(BF16)

    zt = lax.dot_general(wt_ref[...], h, (((1,), (1,)), ((), ())), preferred_element_type=F32)
    cos_t, sin_t = cos_ref[...], sin_ref[...]
    qt = _rope_cols(zt[0:512, :], N_HEADS, cos_t, sin_t) * (HEAD_DIM ** -0.5 * LOG2_E)
    qt_ref[...] = qt.astype(BF16)
    qit_ref[...] = _rope_cols(zt[512:1024, :], N_IDX_HEADS, cos_t, sin_t).astype(BF16)
    vt_ref[0] = zt[1024:1152, :].astype(BF16)
    wit_ref[...] = zt[1152:1160, :] * ((IDX_DIM ** -0.5) * (N_IDX_HEADS ** -0.5))


def _inproj_prompt(x2d, g, wrow, wt, tabs, seq):
    n, d = x2d.shape
    tm = min(SEQ_TILE, seq)
    nt = seq // tm
    grid = (n // tm,)
    row = lambda w: pl.BlockSpec((tm, w), lambda i: (i, 0))
    tab = pl.BlockSpec((tm, LANES), lambda i: (i % nt, 0))
    tab_t = pl.BlockSpec((ROT_HALF, tm), lambda i: (0, i % nt))
    col = lambda r: pl.BlockSpec((r, tm), lambda i: (0, i))
    out_shape = (
        jax.ShapeDtypeStruct((n, 128), F32), jax.ShapeDtypeStruct((n, 128), F32),
        jax.ShapeDtypeStruct((n, IDX_DIM), F32), jax.ShapeDtypeStruct((n, 512), F32),
        jax.ShapeDtypeStruct((n, 128), BF16), jax.ShapeDtypeStruct((n, IDX_DIM), BF16),
        jax.ShapeDtypeStruct((512, n), BF16), jax.ShapeDtypeStruct((512, n), BF16),
        jax.ShapeDtypeStruct((n // tm, 128, tm), BF16),
        jax.ShapeDtypeStruct((N_IDX_HEADS, n), F32),
    )
    out_specs = (row(128), row(128), row(IDX_DIM), row(512), row(128), row(IDX_DIM),
                 col(512), col(512),
                 pl.BlockSpec((1, 128, tm), lambda i: (i, 0, 0)),
                 col(N_IDX_HEADS))
    return pl.pallas_call(
        _inproj_prompt_kernel,
        grid=grid,
        in_specs=[row(d), _const_spec((1, d)), _const_spec(wrow.shape), _const_spec(wt.shape),
                  tab, tab, tab, tab_t, tab_t],
        out_specs=out_specs, out_shape=out_shape,
        compiler_params=pltpu.CompilerParams(dimension_semantics=("arbitrary",), vmem_limit_bytes=VMEM_LIMIT),
        name="inproj_prompt",
    )(x2d, g, wrow, wt, tabs["c"], tabs["sa"], tabs["sb"], tabs["cos_t"], tabs["sin_t"])


def _attn_prompt_kernel(qt_ref, qit_ref, wit_ref, kb_ref, vt_ref, kib_ref, out_ref,
                        keys_ref, qbd_ref, qia_ref, m_ref, acc_ref, *, topk, idx_bits, ck):
    i = pl.program_id(1)
    nch = (i * KEY_BLOCK) // ck + 1

    zero_slab = jnp.zeros((HEAD_DIM, KEY_BLOCK), BF16)
    for h in range(N_HEADS):
        n = h // KV_GROUP
        cols = slice(h * KEY_BLOCK, (h + 1) * KEY_BLOCK)
        qbd_ref[n * HEAD_DIM:(n + 1) * HEAD_DIM, cols] = qt_ref[h * HEAD_DIM:(h + 1) * HEAD_DIM, :]
        qbd_ref[(1 - n) * HEAD_DIM:(2 - n) * HEAD_DIM, cols] = zero_slab
    for h in range(N_IDX_HEADS):
        qia_ref[:, h * KEY_BLOCK:(h + 1) * KEY_BLOCK] = qit_ref[h * IDX_DIM:(h + 1) * IDX_DIM, :]

    w = wit_ref[...]
    row_iota = lax.broadcasted_iota(I32, (ck, KEY_BLOCK), 0)
    q_pos = i * KEY_BLOCK + lax.broadcasted_iota(I32, (ck, KEY_BLOCK), 1)

    def score_body(c, carry):
        off = pl.multiple_of(c * ck, ck)
        dots = jnp.dot(kib_ref[pl.ds(off, ck), :], qia_ref[...], preferred_element_type=F32)
        s = _tree_sum(jnp.maximum(dots[:, h * KEY_BLOCK:(h + 1) * KEY_BLOCK], 0.0) * w[h:h + 1, :]
                      for h in range(N_IDX_HEADS))
        keys_ref[pl.ds(off, ck), :] = jnp.where(off + row_iota <= q_pos, _sortable_key(s), INT_MIN)
        return carry

    lax.fori_loop(0, nch, score_body, 0)

    def count_where(pred):
        def body(c, acc):
            off = pl.multiple_of(c * ck, ck)
            hit = pred(keys_ref[pl.ds(off, ck), :], off)
            return acc + _tree_sum(hit[r * SUBLANES:(r + 1) * SUBLANES, :] for r in range(ck // SUBLANES))
        acc = lax.fori_loop(0, nch, body, jnp.zeros((SUBLANES, KEY_BLOCK), I32))
        return jnp.sum(acc.astype(F32), axis=0, keepdims=True)

    def bit_body(t, thr):
        cand = thr + (jnp.int32(1) << (31 - t))
        cnt = count_where(lambda blk, off: (blk >= cand).astype(I32))
        return jnp.where(cnt >= topk, cand, thr)

    thr = lax.fori_loop(0, 32, bit_body, jnp.full((1, KEY_BLOCK), INT_MIN, I32))
    thr = jnp.maximum(thr, INT_MIN + 1)
    cnt_ge = count_where(lambda blk, off: (blk >= thr).astype(I32))

    @pl.when(jnp.max(cnt_ge) > topk)
    def _():
        cnt_gt = count_where(lambda blk, off: (blk > thr).astype(I32))
        need = topk - cnt_gt

        def idx_body(t, x):
            cand = x + (jnp.int32(1) << (idx_bits - 1 - t))
            cnt = count_where(
                lambda blk, off: jnp.where(blk == thr, (off + row_iota < cand).astype(I32), 0))
            return jnp.where(cnt <= need - 1, cand, x)

        x = lax.fori_loop(0, idx_bits, idx_body, jnp.zeros((1, KEY_BLOCK), I32))

        def drop_body(c, carry):
            off = pl.multiple_of(c * ck, ck)
            blk = keys_ref[pl.ds(off, ck), :]
            keys_ref[pl.ds(off, ck), :] = jnp.where(
                blk == thr, jnp.where(off + row_iota > x, INT_MIN, blk), blk)
            return carry

        lax.fori_loop(0, nch, drop_body, 0)

    m_ref[...] = jnp.full(m_ref.shape, NEG_BIG, F32)
    acc_ref[...] = jnp.zeros(acc_ref.shape, F32)
    ones_rows = jnp.ones((BF16_ROWS, ck), BF16)

    def attn_body(c, carry):
        off = pl.multiple_of(c * ck, ck)
        st = jnp.dot(kb_ref[pl.ds(off, ck), :], qbd_ref[...], preferred_element_type=F32)
        sel = keys_ref[pl.ds(off, ck), :] >= thr
        vt_blk = vt_ref[c]
        m_old = m_ref[...]
        m_new = []
        for n in range(N_KV_HEADS):
            p_cols, alphas = [], []
            for g in range(KV_GROUP):
                cols = slice((n * KV_GROUP + g) * KEY_BLOCK, (n * KV_GROUP + g + 1) * KEY_BLOCK)
                s_c = jnp.where(sel, st[:, cols], NEG_BIG)
                m_c = jnp.maximum(m_old[:, cols], jnp.max(s_c, axis=0, keepdims=True))
                alphas.append(jnp.exp2(m_old[:, cols] - m_c))
                p_cols.append(jnp.exp2(s_c - m_c).astype(BF16))
                m_new.append(m_c)
            p_n = jnp.concatenate(p_cols, axis=1)
            a_n = jnp.concatenate(alphas, axis=1)
            lhs = jnp.concatenate([vt_blk[n * HEAD_DIM:(n + 1) * HEAD_DIM, :], ones_rows], axis=0)
            acc_ref[n] = a_n * acc_ref[n] + jnp.dot(lhs, p_n, preferred_element_type=F32)
        m_ref[...] = jnp.concatenate(m_new, axis=1)
        return carry

    lax.fori_loop(0, nch, attn_body, 0)

    pieces = []
    for h in range(N_HEADS):
        n, g = divmod(h, KV_GROUP)
        cols = slice(g * KEY_BLOCK, (g + 1) * KEY_BLOCK)
        pieces.append(acc_ref[n, 0:HEAD_DIM, cols] / acc_ref[n, HEAD_DIM:HEAD_DIM + 1, cols])
    out_ref[...] = jnp.transpose(jnp.concatenate(pieces, axis=0)).astype(out_ref.dtype)


def _attn_prompt(qt, qit, wit, kb, vt, kib, batch, seq):
    n = batch * seq
    nq = seq // KEY_BLOCK
    ck = min(SEQ_TILE, seq)
    topk = min(TOPK_MAX, seq // 4)
    idx_bits = max(1, int(np.ceil(np.log2(seq))))
    qspec = lambda r: pl.BlockSpec((r, KEY_BLOCK), lambda b, i: (0, b * nq + i))
    kernel = functools.partial(_attn_prompt_kernel, topk=topk, idx_bits=idx_bits, ck=ck)
    return pl.pallas_call(
        kernel,
        grid=(batch, nq),
        in_specs=[qspec(512), qspec(512), qspec(N_IDX_HEADS),
                  pl.BlockSpec((seq, 128), lambda b, i: (b, 0)),
                  pl.BlockSpec((seq // ck, 128, ck), lambda b, i: (b, 0, 0)),
                  pl.BlockSpec((seq, IDX_DIM), lambda b, i: (b, 0))],
        out_specs=pl.BlockSpec((KEY_BLOCK, 512), lambda b, i: (b * nq + i, 0)),
        out_shape=jax.ShapeDtypeStruct((n, 512), BF16),
        scratch_shapes=[pltpu.VMEM((seq, KEY_BLOCK), I32),
                        pltpu.VMEM((128, N_HEADS * KEY_BLOCK), BF16),
                        pltpu.VMEM((IDX_DIM, N_IDX_HEADS * KEY_BLOCK), BF16),
                        pltpu.VMEM((1, N_HEADS * KEY_BLOCK), F32),
                        pltpu.VMEM((N_KV_HEADS, HEAD_DIM + BF16_ROWS, KV_GROUP * KEY_BLOCK), F32)],
        compiler_params=pltpu.CompilerParams(dimension_semantics=("arbitrary", "arbitrary"),
                                             vmem_limit_bytes=VMEM_LIMIT),
        name="attn_prompt",
    )(qt, qit, wit, kb, vt, kib)


def _mix_tail(a_bf, diff, x, wpool_ref, pscale_ref, wout_ref, gpost_ref, gpre_ref, x1_ref, h2_ref):
    outs = []
    for g in range(len(POOL_WINDOWS)):
        cols = slice(g * POOL_GROUP_DIM, (g + 1) * POOL_GROUP_DIM)
        outs.append(jnp.dot(diff[:, cols].astype(BF16), wpool_ref[g], preferred_element_type=F32))
    m = jnp.concatenate(outs, axis=1) * pscale_ref[...]
    mix_in = jnp.concatenate([a_bf, m.astype(BF16)], axis=1)
    mix = jnp.dot(mix_in, wout_ref[...], preferred_element_type=F32)
    x1 = x + _rms(mix, gpost_ref[...])
    x1_ref[...] = x1
    h2_ref[...] = _rms(x1, gpre_ref[...]).astype(BF16)


def _mix_prompt_kernel(a_ref, xp_ref, x_ref, wpool_ref, pscale_ref, wout_ref, gpost_ref, gpre_ref,
                       x1_ref, h2_ref, prev_ref, *, tm, nt):
    i = pl.program_id(0)
    halo = POOL_BUF + 1

    @pl.when(i % nt == 0)
    def _():
        prev_ref[...] = jnp.zeros(prev_ref.shape, F32)

    xp = xp_ref[...]
    ext = jnp.concatenate([prev_ref[...], xp], axis=0)
    prev_ref[...] = xp[tm - halo:, :]
    t = (i % nt) * tm + lax.broadcasted_iota(I32, (tm, 1), 0)
    diffs = []
    for g, win in enumerate(POOL_WINDOWS):
        cols = slice(g * POOL_GROUP_DIM, (g + 1) * POOL_GROUP_DIM)
        s = ext[:, cols]
        sh = 1
        while sh < win:
            s = s + pltpu.roll(s, sh, 0)
            sh *= 2
        cnt = jnp.minimum(t + 1, win).astype(F32)
        diffs.append(s[halo:, :] / cnt - xp[:, cols])
    diff = jnp.concatenate(diffs, axis=1)
    _mix_tail(a_ref[...], diff, x_ref[...], wpool_ref, pscale_ref, wout_ref, gpost_ref, gpre_ref,
              x1_ref, h2_ref)


def _mix_prompt(a, xp, x2d, wpool, pscale, wout, gpost, gpre, seq):
    n, d = x2d.shape
    tm = min(SEQ_TILE, seq)
    nt = seq // tm
    row = lambda w: pl.BlockSpec((tm, w), lambda i: (i, 0))
    return pl.pallas_call(
        functools.partial(_mix_prompt_kernel, tm=tm, nt=nt),
        grid=(n // tm,),
        in_specs=[row(512), row(512), row(d), _const_spec(wpool.shape), _const_spec((1, 512)),
                  _const_spec(wout.shape), _const_spec((1, d)), _const_spec((1, d))],
        out_specs=(row(d), row(d)),
        out_shape=(jax.ShapeDtypeStruct((n, d), F32), jax.ShapeDtypeStruct((n, d), BF16)),
        scratch_shapes=[pltpu.VMEM((POOL_BUF + 1, 512), F32)],
        compiler_params=pltpu.CompilerParams(dimension_semantics=("arbitrary",), vmem_limit_bytes=VMEM_LIMIT),
        name="mix_prompt",
    )(a, xp, x2d, wpool, pscale, wout, gpost, gpre)


def _mix_sample_kernel(a_ref, ext_ref, x_ref, wpool_ref, pscale_ref, wout_ref, gpost_ref, gpre_ref,
                       x1_ref, h2_ref):
    rows = POOL_BUF + 1
    last = ext_ref[rows - 1]
    diffs = []
    for g, win in enumerate(POOL_WINDOWS):
        cols = slice(g * POOL_GROUP_DIM, (g + 1) * POOL_GROUP_DIM)
        wsum = last[:, cols]
        for r in range(rows - win, rows - 1):
            wsum = wsum + ext_ref[r][:, cols]
        diffs.append(wsum / float(win) - last[:, cols])
    diff = jnp.concatenate(diffs, axis=1)
    _mix_tail(a_ref[...].astype(BF16), diff, x_ref[...], wpool_ref, pscale_ref, wout_ref, gpost_ref,
              gpre_ref, x1_ref, h2_ref)


def _mix_sample(a, ext, x2d, wpool, pscale, wout, gpost, gpre):
    n, d = x2d.shape
    full = lambda s: pl.BlockSpec(s, lambda: (0,) * len(s))
    return pl.pallas_call(
        _mix_sample_kernel,
        in_specs=[full(a.shape), full(ext.shape), full(x2d.shape), full(wpool.shape), full((1, 512)),
                  full(wout.shape), full((1, d)), full((1, d))],
        out_specs=(full((n, d)), full((n, d))),
        out_shape=(jax.ShapeDtypeStruct((n, d), F32), jax.ShapeDtypeStruct((n, d), BF16)),
        compiler_params=pltpu.CompilerParams(vmem_limit_bytes=VMEM_LIMIT),
        name="mix_sample",
    )(a, ext, x2d, wpool, pscale, wout, gpost, gpre)


def _ffn_tail(c, val, x1, wdown_ref, gpost_ref, x2_ref):
    y = (jax.nn.gelu(c, approximate=True) * val).astype(BF16)
    f = jnp.dot(y, wdown_ref[...], preferred_element_type=F32)
    x2_ref[...] = x1 + _rms(f, gpost_ref[...])


def _ffn_prompt_kernel(h2_ref, x1_ref, wg_ref, wv_ref, cw_ref, wdown_ref, gpost_ref,
                       x2_ref, cst_ref, prev_ref, *, tm, nt):
    i = pl.program_id(0)

    @pl.when(i % nt == 0)
    def _():
        prev_ref[...] = jnp.zeros(prev_ref.shape, F32)

    h2 = h2_ref[...]
    g = jnp.dot(h2, wg_ref[...], preferred_element_type=F32)
    val = jnp.dot(h2, wv_ref[...], preferred_element_type=F32)
    ext = jnp.concatenate([prev_ref[...], g], axis=0)
    tail = g[tm - SUBLANES:, :]
    prev_ref[...] = tail
    cst_ref[0] = tail
    cw = cw_ref[...]
    g1 = pltpu.roll(ext, 1, 0)[SUBLANES:, :]
    g2 = pltpu.roll(ext, 2, 0)[SUBLANES:, :]
    c = cw[3:4, :] + g2 * cw[0:1, :] + g1 * cw[1:2, :] + g * cw[2:3, :]
    _ffn_tail(c, val, x1_ref[...], wdown_ref, gpost_ref, x2_ref)


def _ffn_prompt(h2, x1, wg, wv, cw, wdown, gpost, batch, seq):
    n, d = x1.shape
    f = wg.shape[1]
    tm = min(256, seq)
    nt = seq // tm
    row = lambda w: pl.BlockSpec((tm, w), lambda i: (i, 0))
    return pl.pallas_call(
        functools.partial(_ffn_prompt_kernel, tm=tm, nt=nt),
        grid=(n // tm,),
        in_specs=[row(d), row(d), _const_spec(wg.shape), _const_spec(wv.shape), _const_spec(cw.shape),
                  _const_spec(wdown.shape), _const_spec((1, d))],
        out_specs=(row(d), pl.BlockSpec((1, SUBLANES, f), lambda i: (i // nt, 0, 0))),
        out_shape=(jax.ShapeDtypeStruct((n, d), F32), jax.ShapeDtypeStruct((batch, SUBLANES, f), F32)),
        scratch_shapes=[pltpu.VMEM((SUBLANES, f), F32)],
        compiler_params=pltpu.CompilerParams(dimension_semantics=("arbitrary",), vmem_limit_bytes=VMEM_LIMIT),
        name="ffn_prompt",
    )(h2, x1, wg, wv, cw, wdown, gpost)


def _ffn_sample_kernel(h2_ref, x1_ref, b0_ref, b1_ref, wg_ref, wv_ref, cw_ref, wdown_ref, gpost_ref,
                       x2_ref, g_ref):
    h2 = h2_ref[...]
    g = jnp.dot(h2, wg_ref[...], preferred_element_type=F32)
    val = jnp.dot(h2, wv_ref[...], preferred_element_type=F32)
    g_ref[...] = g
    cw = cw_ref[...]
    c = cw[3:4, :] + b0_ref[...] * cw[0:1, :] + b1_ref[...] * cw[1:2, :] + g * cw[2:3, :]
    _ffn_tail(c, val, x1_ref[...], wdown_ref, gpost_ref, x2_ref)


def _ffn_sample(h2, x1, b0, b1, wg, wv, cw, wdown, gpost):
    n, d = x1.shape
    f = wg.shape[1]
    full = lambda s: pl.BlockSpec(s, lambda: (0,) * len(s))
    args = (h2, x1, b0, b1, wg, wv, cw, wdown, gpost)
    return pl.pallas_call(
        _ffn_sample_kernel,
        in_specs=[full(a.shape) for a in args],
        out_specs=(full((n, d)), full((n, f))),
        out_shape=(jax.ShapeDtypeStruct((n, d), F32), jax.ShapeDtypeStruct((n, f), F32)),
        compiler_params=pltpu.CompilerParams(vmem_limit_bytes=VMEM_LIMIT),
        name="ffn_sample",
    )(*args)


def _inproj_sample_kernel(x_ref, g_ref, w_ref, c_ref, sa_ref, sb_ref,
                          q_ref, k_ref, v_ref, qi_ref, kiw_ref, xp_ref):
    h = _rms(x_ref[...], g_ref[...]).astype(BF16)
    z = jnp.dot(h, w_ref[...], preferred_element_type=F32)
    c, sa, sb = c_ref[...], sa_ref[...], sb_ref[...]
    rope = lambda lo: _rope_rows(z[:, lo:lo + LANES], c, sa, sb)
    q_ref[...] = jnp.concatenate([rope(j * LANES) for j in range(4)], axis=1) * (HEAD_DIM ** -0.5)
    k_ref[...] = rope(512)
    v_ref[...] = z[:, 640:768]
    qi_ref[...] = jnp.concatenate([rope(768 + j * LANES) for j in range(4)], axis=1)
    lane = lax.broadcasted_iota(I32, (1, LANES), 1)
    kiw = z[:, 1280:1408]
    kiw_ref[...] = jnp.where(lane < IDX_DIM, _rope_rows(kiw, c, sa, sb), kiw)
    xp_ref[...] = z[:, 1408:1920]


def _inproj_sample(x2d, g, wall, tabs):
    n, d = x2d.shape
    full = lambda s: pl.BlockSpec(s, lambda: (0,) * len(s))
    args = (x2d, g, wall, tabs["c"], tabs["sa"], tabs["sb"])
    widths = (512, 128, 128, 512, 128, 512)
    return pl.pallas_call(
        _inproj_sample_kernel,
        in_specs=[full(a.shape) for a in args],
        out_specs=tuple(full((n, w)) for w in widths),
        out_shape=tuple(jax.ShapeDtypeStruct((n, w), F32) for w in widths),
        compiler_params=pltpu.CompilerParams(vmem_limit_bytes=VMEM_LIMIT),
        name="inproj_sample",
    )(*args)


def _page_copy(cache_ref, buf_ref, sem_ref, layer, page, slot, j):
    return pltpu.make_async_copy(cache_ref.at[layer, page], buf_ref.at[slot, j], sem_ref.at[slot])


def _chunk_dma(pt_ref, caches, bufs, sems, layer, b, c, slot, cp, start):
    for j in range(cp):
        page = pt_ref[b, c * cp + j]
        for cache_ref, buf_ref, sem_ref in zip(caches, bufs, sems):
            desc = _page_copy(cache_ref, buf_ref, sem_ref, layer, page, slot, j)
            if start:
                desc.start()
            else:
                desc.wait()


def _pipeline_step(pt_ref, caches, bufs, sems, layer, cp):
    b, c = pl.program_id(0), pl.program_id(1)
    nb, nc = pl.num_programs(0), pl.num_programs(1)
    step = b * nc + c
    slot = step % 2

    @pl.when(step == 0)
    def _():
        _chunk_dma(pt_ref, caches, bufs, sems, layer, b, c, slot, cp, True)

    @pl.when(step + 1 < nb * nc)
    def _():
        nxt = step + 1
        _chunk_dma(pt_ref, caches, bufs, sems, layer, nxt // nc, nxt % nc, 1 - slot, cp, True)

    _chunk_dma(pt_ref, caches, bufs, sems, layer, b, c, slot, cp, False)
    return slot


def _chunk_operand(buf_ref, slot, cp):
    return jnp.concatenate([buf_ref[slot, j] for j in range(cp)], axis=1).astype(BF16)


def _decode_select_kernel(pt_ref, qi_ref, w_ref, kin_ref, cache_ref, bias_ref, bnew_ref,
                          buf_ref, sem_ref, keys_ref, *, layer, cp, topk, idx_bits, past):
    c = pl.program_id(1)
    nc = pl.num_programs(1)
    slot = _pipeline_step(pt_ref, (cache_ref,), (buf_ref,), (sem_ref,), layer, cp)

    qi = qi_ref[0]
    w8 = w_ref[0][:, 0:1]
    dots = jnp.dot(qi, _chunk_operand(buf_ref, slot, cp), preferred_element_type=F32)
    s = jnp.sum(jnp.maximum(dots, 0.0) * w8, axis=0, keepdims=True)
    keys_ref[pl.ds(c, 1), :] = _sortable_key(s)

    @pl.when(c == nc - 1)
    def _():
        ki_new = kin_ref[0][:, 0:IDX_DIM].astype(BF16).astype(F32)
        d_new = jnp.sum(qi.astype(F32) * ki_new, axis=1, keepdims=True)
        key_new = _sortable_key(jnp.sum(jnp.maximum(d_new, 0.0) * w8, axis=0, keepdims=True))
        keys = keys_ref[...]
        one = lambda pred: jnp.where(pred, 1.0, 0.0)
        total = lambda fk, fn: jnp.sum(fk, keepdims=True) + fn

        def bit_body(t, thr):
            cand = thr + (jnp.int32(1) << (31 - t))
            return jnp.where(total(one(keys >= cand), one(key_new >= cand)) >= topk, cand, thr)

        thr = lax.fori_loop(0, 32, bit_body, jnp.full((1, 1), INT_MIN, I32))
        thr = jnp.maximum(thr, INT_MIN + 1)
        need = topk - total(one(keys > thr), one(key_new > thr))

        idx = (lax.broadcasted_iota(I32, keys.shape, 0) * (cp * PAGE_SIZE)
               + lax.broadcasted_iota(I32, keys.shape, 1))
        tie, tie_n = one(keys == thr), one(key_new == thr)

        def idx_body(t, x):
            cand = x + (jnp.int32(1) << (idx_bits - 1 - t))
            cnt = total(tie * one(idx < cand), tie_n * one(past < cand))
            return jnp.where(cnt <= need - 1.0, cand, x)

        x = lax.fori_loop(0, idx_bits, idx_body, jnp.zeros((1, 1), I32))
        sel = jnp.where(keys == thr, one(idx <= x), one(keys > thr))
        sel_n = jnp.where(key_new == thr, one(past <= x), one(key_new > thr))
        bias_ref[0] = (1.0 - sel) * NEG_BIG
        bnew_ref[0] = jnp.broadcast_to((1.0 - sel_n) * NEG_BIG, (1, LANES))


def _decode_attend_kernel(pt_ref, q_ref, bias_ref, bnew_ref, kn_ref, vn_ref, ck_ref, cv_ref, out_ref,
                          kbuf_ref, vbuf_ref, ksem_ref, vsem_ref, m_ref, l_ref, acc_ref, *, layer, cp):
    c = pl.program_id(1)
    nc = pl.num_programs(1)
    slot = _pipeline_step(pt_ref, (ck_ref, cv_ref), (kbuf_ref, vbuf_ref), (ksem_ref, vsem_ref), layer, cp)
    q = q_ref[0]

    @pl.when(c == 0)
    def _():
        k_new = kn_ref[0].astype(BF16).astype(F32)
        s_new = jnp.sum(q.astype(F32) * k_new, axis=1, keepdims=True) + bnew_ref[0][:, 0:1]
        m_ref[...] = jnp.broadcast_to(s_new, m_ref.shape)
        l_ref[...] = jnp.ones(l_ref.shape, F32)
        acc_ref[...] = jnp.broadcast_to(vn_ref[0].astype(BF16).astype(F32), acc_ref.shape)

    kt = _chunk_operand(kbuf_ref, slot, cp)
    vt = _chunk_operand(vbuf_ref, slot, cp)
    s = jnp.dot(q, kt, preferred_element_type=F32) + bias_ref[0, pl.ds(c, 1), :]
    m_old = m_ref[...][:, 0:1]
    m_new = jnp.maximum(m_old, jnp.max(s, axis=1, keepdims=True))
    alpha = jnp.exp(m_old - m_new)
    p = jnp.exp(s - m_new)
    l_ref[...] = alpha * l_ref[...] + jnp.sum(p, axis=1, keepdims=True)
    pv = lax.dot_general(p.astype(BF16), vt, (((1,), (1,)), ((), ())), preferred_element_type=F32)
    acc_ref[...] = alpha * acc_ref[...] + pv
    m_ref[...] = jnp.broadcast_to(m_new, m_ref.shape)

    @pl.when(c == nc - 1)
    def _():
        o = acc_ref[...] / l_ref[...]
        head = lax.broadcasted_iota(I32, (N_HEADS, HEAD_DIM), 0)
        out_ref[0] = jnp.where(head < KV_GROUP, o[:, 0:HEAD_DIM], o[:, HEAD_DIM:])


def _decode_attention(page_table, layer, q, qi, kiw, k_new, v_new, ckt, cvt, cikt):
    nb, n_pages = page_table.shape
    past = n_pages * PAGE_SIZE
    topk = min(TOPK_MAX, (past + 1) // 4)
    idx_bits = int(np.floor(np.log2(past))) + 1
    cp = min(16, n_pages)
    nc = n_pages // cp
    cpk = cp * PAGE_SIZE

    q3 = q.reshape(nb, N_KV_HEADS, KV_GROUP, HEAD_DIM)
    zq = jnp.zeros((nb, KV_GROUP, HEAD_DIM), F32)
    q_bd = jnp.concatenate([jnp.concatenate([q3[:, 0], zq], axis=2),
                            jnp.concatenate([zq, q3[:, 1]], axis=2)], axis=1).astype(BF16)
    qi3 = qi.reshape(nb, N_IDX_HEADS, IDX_DIM).astype(BF16)
    wi = kiw[:, IDX_DIM:IDX_DIM + N_IDX_HEADS] * ((IDX_DIM ** -0.5) * (N_IDX_HEADS ** -0.5))
    w8 = jnp.broadcast_to(wi[:, :, None], (nb, N_IDX_HEADS, LANES))
    kin3, kn3, vn3 = kiw[:, None, :], k_new[:, None, :], v_new[:, None, :]

    seq_blk = lambda s: pl.BlockSpec((1,) + s, lambda b, c, pt: (b,) + (0,) * len(s))
    any_spec = pl.BlockSpec(memory_space=pl.ANY)
    params = pltpu.CompilerParams(dimension_semantics=("arbitrary", "arbitrary"), vmem_limit_bytes=VMEM_LIMIT)

    bias, bnew = pl.pallas_call(
        functools.partial(_decode_select_kernel, layer=layer, cp=cp, topk=topk, idx_bits=idx_bits, past=past),
        grid_spec=pltpu.PrefetchScalarGridSpec(
            num_scalar_prefetch=1, grid=(nb, nc),
            in_specs=[seq_blk((N_IDX_HEADS, IDX_DIM)), seq_blk((N_IDX_HEADS, LANES)), seq_blk((1, LANES)),
                      any_spec],
            out_specs=(seq_blk((nc, cpk)), seq_blk((1, LANES))),
            scratch_shapes=[pltpu.VMEM((2, cp, IDX_DIM, PAGE_SIZE), F32), pltpu.SemaphoreType.DMA((2,)),
                            pltpu.VMEM((nc, cpk), I32)]),
        out_shape=(jax.ShapeDtypeStruct((nb, nc, cpk), F32), jax.ShapeDtypeStruct((nb, 1, LANES), F32)),
        compiler_params=params, name="decode_select",
    )(page_table, qi3, w8, kin3, cikt)

    feat = N_KV_HEADS * HEAD_DIM
    out = pl.pallas_call(
        functools.partial(_decode_attend_kernel, layer=layer, cp=cp),
        grid_spec=pltpu.PrefetchScalarGridSpec(
            num_scalar_prefetch=1, grid=(nb, nc),
            in_specs=[seq_blk((N_HEADS, feat)), seq_blk((nc, cpk)), seq_blk((1, LANES)),
                      seq_blk((1, LANES)), seq_blk((1, LANES)), any_spec, any_spec],
            out_specs=seq_blk((N_HEADS, HEAD_DIM)),
            scratch_shapes=[pltpu.VMEM((2, cp, feat, PAGE_SIZE), F32), pltpu.VMEM((2, cp, feat, PAGE_SIZE), F32),
                            pltpu.SemaphoreType.DMA((2,)), pltpu.SemaphoreType.DMA((2,)),
                            pltpu.VMEM((N_HEADS, LANES), F32), pltpu.VMEM((N_HEADS, LANES), F32),
                            pltpu.VMEM((N_HEADS, feat), F32)]),
        out_shape=jax.ShapeDtypeStruct((nb, N_HEADS, HEAD_DIM), F32),
        compiler_params=params, name="decode_attend",
    )(page_table, q_bd, bias, bnew, kn3, vn3, ckt, cvt)
    return out.reshape(nb, N_HEADS * HEAD_DIM)


def _prep_layer(w_in, w_pool, pool_scale, w_out, w_gate, w_val, conv_w, conv_b, w_down,
                g_pre_mix, g_post_mix, g_pre_ffn, g_post_ffn):
    d = w_in.shape[0]
    sizes = (512, 128, 128, 512, IDX_DIM, N_IDX_HEADS, 512)
    cuts = np.cumsum(sizes)[:-1]
    wq, wk, wv, wqi, wki, wwi, wxp = jnp.split(w_in, [int(c) for c in cuts], axis=1)
    pad = lambda n: jnp.zeros((d, n), w_in.dtype)
    f = w_gate.shape[1]
    cw = jnp.concatenate([conv_w, conv_b[None, :], jnp.zeros((SUBLANES - CONV_WIDTH - 1, f), F32)], axis=0)
    return dict(
        wrow=jnp.concatenate([wk, wv, wki, pad(LANES - IDX_DIM), wxp], axis=1).astype(BF16),
        wt=jnp.concatenate([wq, wqi, wv, wwi, pad(BF16_ROWS - N_IDX_HEADS)], axis=1).T.astype(BF16),
        wall=jnp.concatenate([wq, wk, wv, wqi, wki, wwi, pad(LANES - IDX_DIM - N_IDX_HEADS), wxp],
                             axis=1).astype(BF16),
        wpool=w_pool.astype(BF16), pscale=pool_scale[None, :], wout=w_out.astype(BF16),
        wg=w_gate.astype(BF16), wv=w_val.astype(BF16), cw=cw, wdown=w_down.astype(BF16),
        g_pre_mix=g_pre_mix[None, :], g_post_mix=g_post_mix[None, :],
        g_pre_ffn=g_pre_ffn[None, :], g_post_ffn=g_post_ffn[None, :])


def kernel(x_prompt, x_sample, cache_k, cache_v, cache_idx_k, state_pool, state_conv, page_table,
           g_pre_mix, w_in, w_pool, pool_scale, w_out, g_post_mix, g_pre_ffn,
           w_gate, w_val, conv_w, conv_b, w_down, g_post_ffn):
    bp, sp, d = x_prompt.shape
    bs, ts, _ = x_sample.shape
    assert ts == 1 and sp % KEY_BLOCK == 0 and sp >= 2 * KEY_BLOCK
    depth, n_phys = cache_k.shape[:2]
    past = page_table.shape[1] * PAGE_SIZE
    tabs_p = _rope_tables(jnp.arange(sp))
    tabs_s = _rope_tables(jnp.full((bs,), past, I32))
    feat = N_KV_HEADS * HEAD_DIM
    ckt = jnp.transpose(cache_k, (0, 1, 3, 4, 2)).reshape(depth, n_phys, feat, PAGE_SIZE)
    cvt = jnp.transpose(cache_v, (0, 1, 3, 4, 2)).reshape(depth, n_phys, feat, PAGE_SIZE)
    cikt = jnp.transpose(cache_idx_k, (0, 1, 3, 2))

    yp = x_prompt.reshape(bp * sp, d)
    ys = x_sample.reshape(bs, d)
    outs = [[] for _ in range(10)]
    for l in range(depth):
        p = _prep_layer(w_in[l], w_pool[l], pool_scale[l], w_out[l], w_gate[l], w_val[l], conv_w[l],
                        conv_b[l], w_down[l], g_pre_mix[l], g_post_mix[l], g_pre_ffn[l], g_post_ffn[l])
        k, v, ki, xp, kb, kib, qt, qit, vt, wit = _inproj_prompt(yp, p["g_pre_mix"], p["wrow"], p["wt"], tabs_p, sp)
        a = _attn_prompt(qt, qit, wit, kb, vt, kib, bp, sp)
        x1, h2 = _mix_prompt(a, xp, yp, p["wpool"], p["pscale"], p["wout"], p["g_post_mix"], p["g_pre_ffn"], sp)
        yp, cst = _ffn_prompt(h2, x1, p["wg"], p["wv"], p["cw"], p["wdown"], p["g_post_ffn"], bp, sp)
        outs[0].append(k.reshape(bp, sp, N_KV_HEADS, HEAD_DIM))
        outs[1].append(v.reshape(bp, sp, N_KV_HEADS, HEAD_DIM))
        outs[2].append(ki.reshape(bp, sp, IDX_DIM))
        outs[3].append(xp.reshape(bp, sp, 512)[:, sp - POOL_BUF:, :])
        outs[4].append(cst[:, SUBLANES - (CONV_WIDTH - 1):, :])
        q_s, k_s, v_s, qi_s, kiw_s, xp_s = _inproj_sample(ys, p["g_pre_mix"], p["wall"], tabs_s)
        a_s = _decode_attention(page_table, l, q_s, qi_s, kiw_s, k_s, v_s, ckt, cvt, cikt)
        ext = jnp.concatenate([state_pool[l], xp_s[:, None, :]], axis=1)
        x1_s, h2_s = _mix_sample(a_s, jnp.transpose(ext, (1, 0, 2)), ys, p["wpool"], p["pscale"], p["wout"],
                                 p["g_post_mix"], p["g_pre_ffn"])
        ys, g_s = _ffn_sample(h2_s, x1_s, state_conv[l, :, 0, :], state_conv[l, :, 1, :],
                              p["wg"], p["wv"], p["cw"], p["wdown"], p["g_post_ffn"])
        outs[5].append(k_s.reshape(bs, 1, N_KV_HEADS, HEAD_DIM))
        outs[6].append(v_s.reshape(bs, 1, N_KV_HEADS, HEAD_DIM))
        outs[7].append(kiw_s[:, None, :IDX_DIM])
        outs[8].append(ext[:, 1:, :])
        outs[9].append(jnp.concatenate([state_conv[l, :, 1:, :], g_s[:, None, :]], axis=1))
    return (yp.reshape(bp, sp, d), ys.reshape(bs, 1, d)) + tuple(jnp.stack(o) for o in outs)
```

```python
import functools

import jax
import jax.numpy as jnp
import numpy as np
from jax import lax
from jax.experimental import pallas as pl
from jax.experimental.pallas import tpu as pltpu

F32 = jnp.float32
BF16 = jnp.bfloat16
I32 = jnp.int32
I16 = jnp.int16

N_HEADS = 8
N_KV_HEADS = 2
KV_GROUP = N_HEADS // N_KV_HEADS
HEAD_DIM = 64
N_IDX_HEADS = 8
IDX_DIM = 64
ROPE_THETA = 500000.0
ROT_HALF = HEAD_DIM // 8
TOPK_MAX = 256
PAGE_SIZE = 128
POOL_WINDOWS = (2, 4, 8, 16)
POOL_GROUP_DIM = 128
POOL_BUF = max(POOL_WINDOWS) - 1
CONV_WIDTH = 3
RMS_EPS = 1e-6

LANES = 128
SUBLANES = 8
BF16_ROWS = 16
KEY_BLOCK = 128
SEQ_TILE = 512
ATTN_CHUNK = 512
DECODE_CHUNK_PAGES = 32
VMEM_LIMIT = 56 * 1024 * 1024
LOG2_E = 1.4426950408889634

INT_MIN = np.int32(-2147483648)
HALF_BIAS = 32768
NEG_BIG = -1e30
UNDERFLOW_GUARD = 1e-30
FEAT = N_KV_HEADS * HEAD_DIM


def _const_spec(shape):
    nd = len(shape)
    return pl.BlockSpec(shape, lambda *_: (0,) * nd, pipeline_mode=pl.Buffered(1))


def _rms(x, g):
    ms = jnp.mean(x * x, axis=-1, keepdims=True)
    return x * lax.rsqrt(ms + RMS_EPS) * g


def _sortable_key(s):
    bits = lax.bitcast_convert_type(s, I32)
    return bits ^ ((bits >> 31) & np.int32(0x7FFFFFFF))


def _tree_sum(parts):
    parts = list(parts)
    while len(parts) > 1:
        nxt = [parts[k] + parts[k + 1] for k in range(0, len(parts) - 1, 2)]
        if len(parts) % 2:
            nxt.append(parts[-1])
        parts = nxt
    return parts[0]


def _rope_tables(pos):
    inv = ROPE_THETA ** (-jnp.arange(ROT_HALF, dtype=F32) / ROT_HALF)
    ang = pos.astype(F32)[:, None] * inv[None, :]
    cos, sin = jnp.cos(ang), jnp.sin(ang)
    p = pos.shape[0]
    one = jnp.ones((p, HEAD_DIM - 2 * ROT_HALF), F32)
    zero = jnp.zeros((p, HEAD_DIM - 2 * ROT_HALF), F32)
    z8 = jnp.zeros((p, ROT_HALF), F32)
    c64 = jnp.concatenate([cos, cos, one], axis=1)
    sa64 = jnp.concatenate([-sin, z8, zero], axis=1)
    sb64 = jnp.concatenate([z8, sin, zero], axis=1)
    rep = LANES // HEAD_DIM
    return dict(c=jnp.tile(c64, (1, rep)), sa=jnp.tile(sa64, (1, rep)), sb=jnp.tile(sb64, (1, rep)),
                cos_t=cos.T, sin_t=sin.T)


def _rope_rows(zc, c, sa, sb):
    return zc * c + pltpu.roll(zc, LANES - ROT_HALF, 1) * sa + pltpu.roll(zc, ROT_HALF, 1) * sb


def _rope_cols(zt, n_heads, cos_t, sin_t):
    pieces = []
    for h in range(n_heads):
        b = h * HEAD_DIM
        x1 = zt[b:b + ROT_HALF, :]
        x2 = zt[b + ROT_HALF:b + 2 * ROT_HALF, :]
        pieces.append(x1 * cos_t - x2 * sin_t)
        pieces.append(x2 * cos_t + x1 * sin_t)
        pieces.append(zt[b + 2 * ROT_HALF:b + HEAD_DIM, :])
    return jnp.concatenate(pieces, axis=0)


def _inproj_prompt_kernel(x_ref, g_ref, wrow_ref, wt_ref, c_ref, sa_ref, sb_ref, cos_ref, sin_ref,
                          k_ref, v_ref, ki_ref, xp_ref, kb_ref, kib_ref, qt_ref, qit_ref, vt_ref, wit_ref):
    h = _rms(x_ref[...], g_ref[...]).astype(BF16)
    z = jnp.dot(h, wrow_ref[...], preferred_element_type=F32)
    c, sa, sb = c_ref[...], sa_ref[...], sb_ref[...]
    k = _rope_rows(z[:, 0:128], c, sa, sb)
    v = z[:, 128:256]
    ki = _rope_rows(z[:, 256:384], c, sa, sb)[:, :IDX_DIM]
    k_ref[...] = k
    v_ref[...] = v
    ki_ref[...] = ki
    xp_ref[...] = z[:, 384:896]
    one_col = jnp.where(lax.broadcasted_iota(I32, k.shape, 1) == 0, 1.0, 0.0)
    kb_ref[...] = jnp.concatenate([k, one_col], axis=1).astype(BF16)
    kib_ref[...] = ki.astype(BF16)

    zt = lax.dot_general(wt_ref[...], h, (((1,), (1,)), ((), ())), preferred_element_type=F32)
    cos_t, sin_t = cos_ref[...], sin_ref[...]
    qt = _rope_cols(zt[0:512, :], N_HEADS, cos_t, sin_t) * (HEAD_DIM ** -0.5 * LOG2_E)
    qt_ref[...] = qt.astype(BF16)
    qit_ref[...] = _rope_cols(zt[512:1024, :], N_IDX_HEADS, cos_t, sin_t).astype(BF16)
    vt = zt[1024:1152, :].astype(BF16)
    ck = vt_ref.shape[2]
    for j in range(vt_ref.shape[0]):
        vt_ref[j] = vt[:, j * ck:(j + 1) * ck]
    wit_ref[...] = zt[1152:1160, :] * ((IDX_DIM ** -0.5) * (N_IDX_HEADS ** -0.5))


def _inproj_prompt(x2d, g, wrow, wt, tabs, seq):
    n, d = x2d.shape
    tm = min(SEQ_TILE, seq)
    ck = min(ATTN_CHUNK, seq)
    nt = seq // tm
    grid = (n // tm,)
    row = lambda w: pl.BlockSpec((tm, w), lambda i: (i, 0))
    tab = pl.BlockSpec((tm, LANES), lambda i: (i % nt, 0))
    tab_t = pl.BlockSpec((ROT_HALF, tm), lambda i: (0, i % nt))
    col = lambda r: pl.BlockSpec((r, tm), lambda i: (0, i))
    out_shape = (
        jax.ShapeDtypeStruct((n, 128), F32), jax.ShapeDtypeStruct((n, 128), F32),
        jax.ShapeDtypeStruct((n, IDX_DIM), F32), jax.ShapeDtypeStruct((n, 512), F32),
        jax.ShapeDtypeStruct((n, 2 * FEAT), BF16), jax.ShapeDtypeStruct((n, IDX_DIM), BF16),
        jax.ShapeDtypeStruct((512, n), BF16), jax.ShapeDtypeStruct((512, n), BF16),
        jax.ShapeDtypeStruct((n // ck, 128, ck), BF16),
        jax.ShapeDtypeStruct((N_IDX_HEADS, n), F32),
    )
    out_specs = (row(128), row(128), row(IDX_DIM), row(512), row(2 * FEAT), row(IDX_DIM),
                 col(512), col(512),
                 pl.BlockSpec((tm // ck, 128, ck), lambda i: (i, 0, 0)),
                 col(N_IDX_HEADS))
    return pl.pallas_call(
        _inproj_prompt_kernel,
        grid=grid,
        in_specs=[row(d), _const_spec((1, d)), _const_spec(wrow.shape), _const_spec(wt.shape),
                  tab, tab, tab, tab_t, tab_t],
        out_specs=out_specs, out_shape=out_shape,
        compiler_params=pltpu.CompilerParams(dimension_semantics=("arbitrary",), vmem_limit_bytes=VMEM_LIMIT),
        name="inproj_prompt",
    )(x2d, g, wrow, wt, tabs["c"], tabs["sa"], tabs["sb"], tabs["cos_t"], tabs["sin_t"])


def _attn_prompt_kernel(qt_ref, qit_ref, wit_ref, kb_ref, vt_ref, kib_ref, out_ref,
                        keys_ref, hi_ref, lo_ref, lom_ref, qbd_ref, qia_ref, kmax_ref, m_ref, acc_ref,
                        *, topk, idx_bits, ck):
    i = pl.program_id(1)
    nch = (i * KEY_BLOCK) // ck + 1

    @pl.when(i == 0)
    def _():
        def kmax_body(c, mx):
            off = pl.multiple_of(c * ck, ck)
            kf = kb_ref[pl.ds(off, ck), 0:FEAT].astype(F32)
            sq = kf * kf
            return tuple(jnp.maximum(mx[n], jnp.max(jnp.sum(sq[:, n * HEAD_DIM:(n + 1) * HEAD_DIM], axis=1,
                                                            keepdims=True), axis=0, keepdims=True))
                         for n in range(N_KV_HEADS))
        mx = lax.fori_loop(0, kb_ref.shape[0] // ck, kmax_body, (jnp.zeros((1, 1), F32),) * N_KV_HEADS)
        for n in range(N_KV_HEADS):
            kmax_ref[n:n + 1, :] = jnp.broadcast_to(jnp.sqrt(mx[n]), (1, KEY_BLOCK))

    zero_slab = jnp.zeros((HEAD_DIM, KEY_BLOCK), BF16)
    shift = []
    for h in range(N_HEADS):
        n = h // KV_GROUP
        cols = slice(h * KEY_BLOCK, (h + 1) * KEY_BLOCK)
        q_h = qt_ref[h * HEAD_DIM:(h + 1) * HEAD_DIM, :]
        qbd_ref[n * HEAD_DIM:(n + 1) * HEAD_DIM, cols] = q_h
        qbd_ref[(1 - n) * HEAD_DIM:(2 - n) * HEAD_DIM, cols] = zero_slab
        q_f = q_h.astype(F32)
        shift.append(-jnp.sqrt(jnp.sum(q_f * q_f, axis=0, keepdims=True)) * kmax_ref[n:n + 1, :])
    first_row = lax.broadcasted_iota(I32, (BF16_ROWS, N_HEADS * KEY_BLOCK), 0) == 0
    qbd_ref[FEAT:FEAT + BF16_ROWS, :] = jnp.where(first_row, jnp.concatenate(shift, axis=1), 0.0).astype(BF16)
    qbd_ref[FEAT + BF16_ROWS:, :] = jnp.zeros((FEAT - BF16_ROWS, N_HEADS * KEY_BLOCK), BF16)
    for h in range(N_IDX_HEADS):
        qia_ref[:, h * KEY_BLOCK:(h + 1) * KEY_BLOCK] = qit_ref[h * IDX_DIM:(h + 1) * IDX_DIM, :]

    w = wit_ref[...]
    row_iota = lax.broadcasted_iota(I32, (ck, KEY_BLOCK), 0)
    q_pos = i * KEY_BLOCK + lax.broadcasted_iota(I32, (ck, KEY_BLOCK), 1)

    def score_body(c, carry):
        off = pl.multiple_of(c * ck, ck)
        kic = kib_ref[pl.ds(off, ck), :]
        terms = []
        for j in range(N_IDX_HEADS // 2):
            dots = jnp.dot(kic, qia_ref[:, j * 2 * KEY_BLOCK:(j + 1) * 2 * KEY_BLOCK],
                           preferred_element_type=F32)
            terms.append(jnp.maximum(dots[:, 0:KEY_BLOCK], 0.0) * w[2 * j:2 * j + 1, :])
            terms.append(jnp.maximum(dots[:, KEY_BLOCK:], 0.0) * w[2 * j + 1:2 * j + 2, :])
        key = jnp.where(off + row_iota <= q_pos, _sortable_key(_tree_sum(terms)), INT_MIN)
        keys_ref[pl.ds(off, ck), :] = key
        hi_ref[pl.ds(off, ck), :] = (key >> 16).astype(I16)
        lo_ref[pl.ds(off, ck), :] = ((key & 0xFFFF) - HALF_BIAS).astype(I16)
        return carry

    lax.fori_loop(0, nch, score_body, 0)

    def count_chunks(ref, rows, pred):
        def body(c, acc):
            off = pl.multiple_of(c * ck, ck)
            hit = pred(ref[pl.ds(off, ck), :], off)
            part = _tree_sum(hit[r * rows:(r + 1) * rows, :] for r in range(ck // rows))
            return acc + part.astype(F32)
        acc = lax.fori_loop(0, nch, body, jnp.zeros((rows, KEY_BLOCK), F32))
        return jnp.sum(acc, axis=0, keepdims=True)

    count_where = functools.partial(count_chunks, keys_ref, SUBLANES)
    one16, zero16 = jnp.int16(1), jnp.int16(0)

    def search16(ref, base):
        def bit_body(t, thr):
            cand = thr + (jnp.int32(1) << (15 - t))
            cand16 = cand.astype(I16)
            cnt = count_chunks(ref, BF16_ROWS, lambda blk, off: jnp.where(blk >= cand16, one16, zero16))
            return jnp.where(base + cnt >= topk, cand, thr)
        return lax.fori_loop(0, 16, bit_body, jnp.full((1, KEY_BLOCK), -HALF_BIAS, I32))

    thr_hi = search16(hi_ref, 0.0)
    thr_hi16 = thr_hi.astype(I16)
    cnt_above = count_chunks(hi_ref, BF16_ROWS, lambda blk, off: jnp.where(blk > thr_hi16, one16, zero16))

    def bucket_body(c, carry):
        off = pl.multiple_of(c * ck, ck)
        lom_ref[pl.ds(off, ck), :] = jnp.where(hi_ref[pl.ds(off, ck), :] == thr_hi16,
                                               lo_ref[pl.ds(off, ck), :], jnp.int16(-HALF_BIAS))
        return carry

    lax.fori_loop(0, nch, bucket_body, 0)
    thr_lo = search16(lom_ref, cnt_above)
    thr = thr_hi * (2 * HALF_BIAS) + (thr_lo + HALF_BIAS)
    thr = jnp.maximum(thr, INT_MIN + 1)
    cnt_ge = count_where(lambda blk, off: (blk >= thr).astype(I32))

    @pl.when(jnp.max(cnt_ge) > topk)
    def _():
        cnt_gt = count_where(lambda blk, off: (blk > thr).astype(I32))
        need = topk - cnt_gt

        def idx_body(t, x):
            cand = x + (jnp.int32(1) << (idx_bits - 1 - t))
            cnt = count_where(
                lambda blk, off: jnp.where(blk == thr, (off + row_iota < cand).astype(I32), 0))
            return jnp.where(cnt <= need - 1, cand, x)

        x = lax.fori_loop(0, idx_bits, idx_body, jnp.zeros((1, KEY_BLOCK), I32))

        def drop_body(c, carry):
            off = pl.multiple_of(c * ck, ck)
            blk = keys_ref[pl.ds(off, ck), :]
            keys_ref[pl.ds(off, ck), :] = jnp.where(
                blk == thr, jnp.where(off + row_iota > x, INT_MIN, blk), blk)
            return carry

        lax.fori_loop(0, nch, drop_body, 0)

    ones_rows = jnp.ones((BF16_ROWS, ck), BF16)
    acc_ref[...] = jnp.zeros(acc_ref.shape, F32)

    def fast_body(c, carry):
        off = pl.multiple_of(c * ck, ck)
        st = jnp.dot(kb_ref[pl.ds(off, ck), :], qbd_ref[...], preferred_element_type=F32)
        sel = keys_ref[pl.ds(off, ck), :] >= thr
        vt_blk = vt_ref[c]
        for n in range(N_KV_HEADS):
            p_n = jnp.concatenate(
                [jnp.where(sel, jnp.exp2(st[:, h * KEY_BLOCK:(h + 1) * KEY_BLOCK]), 0.0).astype(BF16)
                 for h in range(n * KV_GROUP, (n + 1) * KV_GROUP)], axis=1)
            lhs = jnp.concatenate([vt_blk[n * HEAD_DIM:(n + 1) * HEAD_DIM, :], ones_rows], axis=0)
            acc_ref[n] += jnp.dot(lhs, p_n, preferred_element_type=F32)
        return carry

    lax.fori_loop(0, nch, fast_body, 0)
    l_min = jnp.minimum(jnp.min(acc_ref[0, HEAD_DIM:HEAD_DIM + 1, :]), jnp.min(acc_ref[1, HEAD_DIM:HEAD_DIM + 1, :]))

    def online_body(c, carry):
        off = pl.multiple_of(c * ck, ck)
        st = jnp.dot(kb_ref[pl.ds(off, ck), :], qbd_ref[...], preferred_element_type=F32)
        sel = keys_ref[pl.ds(off, ck), :] >= thr
        vt_blk = vt_ref[c]
        m_old = m_ref[...]
        m_new = []
        for n in range(N_KV_HEADS):
            p_cols, alphas = [], []
            for g in range(KV_GROUP):
                h = n * KV_GROUP + g
                cols = slice(h * KEY_BLOCK, (h + 1) * KEY_BLOCK)
                s_c = jnp.where(sel, st[:, cols], NEG_BIG)
                m_c = jnp.maximum(m_old[:, cols], jnp.max(s_c, axis=0, keepdims=True))
                alphas.append(jnp.exp2(m_old[:, cols] - m_c))
                p_cols.append(jnp.exp2(s_c - m_c).astype(BF16))
                m_new.append(m_c)
            p_n = jnp.concatenate(p_cols, axis=1)
            a_n = jnp.concatenate(alphas, axis=1)
            lhs = jnp.concatenate([vt_blk[n * HEAD_DIM:(n + 1) * HEAD_DIM, :], ones_rows], axis=0)
            acc_ref[n] = a_n * acc_ref[n] + jnp.dot(lhs, p_n, preferred_element_type=F32)
        m_ref[...] = jnp.concatenate(m_new, axis=1)
        return carry

    @pl.when(jnp.logical_not(l_min > UNDERFLOW_GUARD))
    def _():
        m_ref[...] = jnp.full(m_ref.shape, NEG_BIG, F32)
        acc_ref[...] = jnp.zeros(acc_ref.shape, F32)
        lax.fori_loop(0, nch, online_body, 0)

    pieces = []
    for h in range(N_HEADS):
        n, g = divmod(h, KV_GROUP)
        cols = slice(g * KEY_BLOCK, (g + 1) * KEY_BLOCK)
        pieces.append(acc_ref[n, 0:HEAD_DIM, cols] / acc_ref[n, HEAD_DIM:HEAD_DIM + 1, cols])
    out_ref[...] = jnp.transpose(jnp.concatenate(pieces, axis=0)).astype(out_ref.dtype)


def _attn_prompt(qt, qit, wit, kb, vt, kib, batch, seq):
    n = batch * seq
    nq = seq // KEY_BLOCK
    ck = min(ATTN_CHUNK, seq)
    topk = min(TOPK_MAX, seq // 4)
    idx_bits = max(1, int(np.ceil(np.log2(seq))))
    qspec = lambda r: pl.BlockSpec((r, KEY_BLOCK), lambda b, i: (0, b * nq + i))
    kernel = functools.partial(_attn_prompt_kernel, topk=topk, idx_bits=idx_bits, ck=ck)
    return pl.pallas_call(
        kernel,
        grid=(batch, nq),
        in_specs=[qspec(512), qspec(512), qspec(N_IDX_HEADS),
                  pl.BlockSpec((seq, 2 * FEAT), lambda b, i: (b, 0)),
                  pl.BlockSpec((seq // ck, 128, ck), lambda b, i: (b, 0, 0)),
                  pl.BlockSpec((seq, IDX_DIM), lambda b, i: (b, 0))],
        out_specs=pl.BlockSpec((KEY_BLOCK, 512), lambda b, i: (b * nq + i, 0)),
        out_shape=jax.ShapeDtypeStruct((n, 512), BF16),
        scratch_shapes=[pltpu.VMEM((seq, KEY_BLOCK), I32),
                        pltpu.VMEM((seq, KEY_BLOCK), I16), pltpu.VMEM((seq, KEY_BLOCK), I16),
                        pltpu.VMEM((seq, KEY_BLOCK), I16),
                        pltpu.VMEM((2 * FEAT, N_HEADS * KEY_BLOCK), BF16),
                        pltpu.VMEM((IDX_DIM, N_IDX_HEADS * KEY_BLOCK), BF16),
                        pltpu.VMEM((N_KV_HEADS, KEY_BLOCK), F32),
                        pltpu.VMEM((1, N_HEADS * KEY_BLOCK), F32),
                        pltpu.VMEM((N_KV_HEADS, HEAD_DIM + BF16_ROWS, KV_GROUP * KEY_BLOCK), F32)],
        compiler_params=pltpu.CompilerParams(dimension_semantics=("arbitrary", "arbitrary"),
                                             vmem_limit_bytes=VMEM_LIMIT),
        name="attn_prompt",
    )(qt, qit, wit, kb, vt, kib)


def _mix_tail(a_bf, diff, x, wpool_ref, pscale_ref, wout_ref, gpost_ref, gpre_ref, x1_ref, h2_ref):
    outs = []
    for g in range(len(POOL_WINDOWS)):
        cols = slice(g * POOL_GROUP_DIM, (g + 1) * POOL_GROUP_DIM)
        outs.append(jnp.dot(diff[:, cols].astype(BF16), wpool_ref[g], preferred_element_type=F32))
    m = jnp.concatenate(outs, axis=1) * pscale_ref[...]
    mix_in = jnp.concatenate([a_bf, m.astype(BF16)], axis=1)
    mix = jnp.dot(mix_in, wout_ref[...], preferred_element_type=F32)
    x1 = x + _rms(mix, gpost_ref[...])
    x1_ref[...] = x1
    h2_ref[...] = _rms(x1, gpre_ref[...]).astype(BF16)


def _mix_prompt_kernel(a_ref, xp_ref, x_ref, wpool_ref, pscale_ref, wout_ref, gpost_ref, gpre_ref,
                       x1_ref, h2_ref, prev_ref, *, tm, nt):
    i = pl.program_id(0)
    halo = POOL_BUF + 1

    @pl.when(i % nt == 0)
    def _():
        prev_ref[...] = jnp.zeros(prev_ref.shape, F32)

    xp = xp_ref[...]
    ext = jnp.concatenate([prev_ref[...], xp], axis=0)
    prev_ref[...] = xp[tm - halo:, :]
    t = (i % nt) * tm + lax.broadcasted_iota(I32, (tm, 1), 0)
    diffs = []
    for g, win in enumerate(POOL_WINDOWS):
        cols = slice(g * POOL_GROUP_DIM, (g + 1) * POOL_GROUP_DIM)
        s = ext[:, cols]
        sh = 1
        while sh < win:
            s = s + pltpu.roll(s, sh, 0)
            sh *= 2
        cnt = jnp.minimum(t + 1, win).astype(F32)
        diffs.append(s[halo:, :] / cnt - xp[:, cols])
    diff = jnp.concatenate(diffs, axis=1)
    _mix_tail(a_ref[...], diff, x_ref[...], wpool_ref, pscale_ref, wout_ref, gpost_ref, gpre_ref,
              x1_ref, h2_ref)


def _mix_prompt(a, xp, x2d, wpool, pscale, wout, gpost, gpre, seq):
    n, d = x2d.shape
    tm = min(SEQ_TILE, seq)
    nt = seq // tm
    row = lambda w: pl.BlockSpec((tm, w), lambda i: (i, 0))
    return pl.pallas_call(
        functools.partial(_mix_prompt_kernel, tm=tm, nt=nt),
        grid=(n // tm,),
        in_specs=[row(512), row(512), row(d), _const_spec(wpool.shape), _const_spec((1, 512)),
                  _const_spec(wout.shape), _const_spec((1, d)), _const_spec((1, d))],
        out_specs=(row(d), row(d)),
        out_shape=(jax.ShapeDtypeStruct((n, d), F32), jax.ShapeDtypeStruct((n, d), BF16)),
        scratch_shapes=[pltpu.VMEM((POOL_BUF + 1, 512), F32)],
        compiler_params=pltpu.CompilerParams(dimension_semantics=("arbitrary",), vmem_limit_bytes=VMEM_LIMIT),
        name="mix_prompt",
    )(a, xp, x2d, wpool, pscale, wout, gpost, gpre)


def _mix_sample_kernel(a_ref, ext_ref, x_ref, wpool_ref, pscale_ref, wout_ref, gpost_ref, gpre_ref,
                       x1_ref, h2_ref):
    rows = POOL_BUF + 1
    last = ext_ref[rows - 1]
    diffs = []
    for g, win in enumerate(POOL_WINDOWS):
        cols = slice(g * POOL_GROUP_DIM, (g + 1) * POOL_GROUP_DIM)
        wsum = last[:, cols]
        for r in range(rows - win, rows - 1):
            wsum = wsum + ext_ref[r][:, cols]
        diffs.append(wsum / float(win) - last[:, cols])
    diff = jnp.concatenate(diffs, axis=1)
    _mix_tail(a_ref[...].astype(BF16), diff, x_ref[...], wpool_ref, pscale_ref, wout_ref, gpost_ref,
              gpre_ref, x1_ref, h2_ref)


def _mix_sample(a, ext, x2d, wpool, pscale, wout, gpost, gpre):
    n, d = x2d.shape
    full = lambda s: pl.BlockSpec(s, lambda: (0,) * len(s))
    return pl.pallas_call(
        _mix_sample_kernel,
        in_specs=[full(a.shape), full(ext.shape), full(x2d.shape), full(wpool.shape), full((1, 512)),
                  full(wout.shape), full((1, d)), full((1, d))],
        out_specs=(full((n, d)), full((n, d))),
        out_shape=(jax.ShapeDtypeStruct((n, d), F32), jax.ShapeDtypeStruct((n, d), BF16)),
        compiler_params=pltpu.CompilerParams(vmem_limit_bytes=VMEM_LIMIT),
        name="mix_sample",
    )(a, ext, x2d, wpool, pscale, wout, gpost, gpre)


def _ffn_tail(c, val, x1, wdown_ref, gpost_ref, x2_ref):
    y = (jax.nn.gelu(c, approximate=True) * val).astype(BF16)
    f = jnp.dot(y, wdown_ref[...], preferred_element_type=F32)
    x2_ref[...] = x1 + _rms(f, gpost_ref[...])


def _ffn_prompt_kernel(h2_ref, x1_ref, wg_ref, wv_ref, cw_ref, wdown_ref, gpost_ref,
                       x2_ref, cst_ref, prev_ref, *, tm, nt):
    i = pl.program_id(0)

    @pl.when(i % nt == 0)
    def _():
        prev_ref[...] = jnp.zeros(prev_ref.shape, F32)

    h2 = h2_ref[...]
    g = jnp.dot(h2, wg_ref[...], preferred_element_type=F32)
    val = jnp.dot(h2, wv_ref[...], preferred_element_type=F32)
    ext = jnp.concatenate([prev_ref[...], g], axis=0)
    tail = g[tm - SUBLANES:, :]
    prev_ref[...] = tail
    cst_ref[0] = tail
    cw = cw_ref[...]
    g1 = pltpu.roll(ext, 1, 0)[SUBLANES:, :]
    g2 = pltpu.roll(ext, 2, 0)[SUBLANES:, :]
    c = cw[3:4, :] + g2 * cw[0:1, :] + g1 * cw[1:2, :] + g * cw[2:3, :]
    _ffn_tail(c, val, x1_ref[...], wdown_ref, gpost_ref, x2_ref)


def _ffn_prompt(h2, x1, wg, wv, cw, wdown, gpost, batch, seq):
    n, d = x1.shape
    f = wg.shape[1]
    tm = min(256, seq)
    nt = seq // tm
    row = lambda w: pl.BlockSpec((tm, w), lambda i: (i, 0))
    return pl.pallas_call(
        functools.partial(_ffn_prompt_kernel, tm=tm, nt=nt),
        grid=(n // tm,),
        in_specs=[row(d), row(d), _const_spec(wg.shape), _const_spec(wv.shape), _const_spec(cw.shape),
                  _const_spec(wdown.shape), _const_spec((1, d))],
        out_specs=(row(d), pl.BlockSpec((1, SUBLANES, f), lambda i: (i // nt, 0, 0))),
        out_shape=(jax.ShapeDtypeStruct((n, d), F32), jax.ShapeDtypeStruct((batch, SUBLANES, f), F32)),
        scratch_shapes=[pltpu.VMEM((SUBLANES, f), F32)],
        compiler_params=pltpu.CompilerParams(dimension_semantics=("arbitrary",), vmem_limit_bytes=VMEM_LIMIT),
        name="ffn_prompt",
    )(h2, x1, wg, wv, cw, wdown, gpost)


def _ffn_sample_kernel(h2_ref, x1_ref, b0_ref, b1_ref, wg_ref, wv_ref, cw_ref, wdown_ref, gpost_ref,
                       x2_ref, g_ref):
    h2 = h2_ref[...]
    g = jnp.dot(h2, wg_ref[...], preferred_element_type=F32)
    val = jnp.dot(h2, wv_ref[...], preferred_element_type=F32)
    g_ref[...] = g
    cw = cw_ref[...]
    c = cw[3:4, :] + b0_ref[...] * cw[0:1, :] + b1_ref[...] * cw[1:2, :] + g * cw[2:3, :]
    _ffn_tail(c, val, x1_ref[...], wdown_ref, gpost_ref, x2_ref)


def _ffn_sample(h2, x1, b0, b1, wg, wv, cw, wdown, gpost):
    n, d = x1.shape
    f = wg.shape[1]
    full = lambda s: pl.BlockSpec(s, lambda: (0,) * len(s))
    args = (h2, x1, b0, b1, wg, wv, cw, wdown, gpost)
    return pl.pallas_call(
        _ffn_sample_kernel,
        in_specs=[full(a.shape) for a in args],
        out_specs=(full((n, d)), full((n, f))),
        out_shape=(jax.ShapeDtypeStruct((n, d), F32), jax.ShapeDtypeStruct((n, f), F32)),
        compiler_params=pltpu.CompilerParams(vmem_limit_bytes=VMEM_LIMIT),
        name="ffn_sample",
    )(*args)


def _inproj_sample_kernel(x_ref, g_ref, w_ref, c_ref, sa_ref, sb_ref,
                          q_ref, k_ref, v_ref, qi_ref, kiw_ref, xp_ref):
    h = _rms(x_ref[...], g_ref[...]).astype(BF16)
    z = jnp.dot(h, w_ref[...], preferred_element_type=F32)
    c, sa, sb = c_ref[...], sa_ref[...], sb_ref[...]
    rope = lambda lo: _rope_rows(z[:, lo:lo + LANES], c, sa, sb)
    q_ref[...] = jnp.concatenate([rope(j * LANES) for j in range(4)], axis=1) * (HEAD_DIM ** -0.5)
    k_ref[...] = rope(512)
    v_ref[...] = z[:, 640:768]
    qi_ref[...] = jnp.concatenate([rope(768 + j * LANES) for j in range(4)], axis=1)
    lane = lax.broadcasted_iota(I32, (1, LANES), 1)
    kiw = z[:, 1280:1408]
    kiw_ref[...] = jnp.where(lane < IDX_DIM, _rope_rows(kiw, c, sa, sb), kiw)
    xp_ref[...] = z[:, 1408:1920]


def _inproj_sample(x2d, g, wall, tabs):
    n, d = x2d.shape
    full = lambda s: pl.BlockSpec(s, lambda: (0,) * len(s))
    args = (x2d, g, wall, tabs["c"], tabs["sa"], tabs["sb"])
    widths = (512, 128, 128, 512, 128, 512)
    return pl.pallas_call(
        _inproj_sample_kernel,
        in_specs=[full(a.shape) for a in args],
        out_specs=tuple(full((n, w)) for w in widths),
        out_shape=tuple(jax.ShapeDtypeStruct((n, w), F32) for w in widths),
        compiler_params=pltpu.CompilerParams(vmem_limit_bytes=VMEM_LIMIT),
        name="inproj_sample",
    )(*args)


def _page_copy(cache_ref, buf_ref, sem_ref, layer, page, slot, j):
    return pltpu.make_async_copy(cache_ref.at[layer, page], buf_ref.at[slot, j], sem_ref.at[slot])


def _chunk_dma(pt_ref, caches, bufs, sems, layer, b, c, slot, cp, start):
    for j in range(cp):
        page = pt_ref[b, c * cp + j]
        for cache_ref, buf_ref, sem_ref in zip(caches, bufs, sems):
            desc = _page_copy(cache_ref, buf_ref, sem_ref, layer, page, slot, j)
            if start:
                desc.start()
            else:
                desc.wait()


def _pipeline_step(pt_ref, caches, bufs, sems, layer, cp):
    b, c = pl.program_id(0), pl.program_id(1)
    nb, nc = pl.num_programs(0), pl.num_programs(1)
    step = b * nc + c
    slot = step % 2

    @pl.when(step == 0)
    def _():
        _chunk_dma(pt_ref, caches, bufs, sems, layer, b, c, slot, cp, True)

    @pl.when(step + 1 < nb * nc)
    def _():
        nxt = step + 1
        _chunk_dma(pt_ref, caches, bufs, sems, layer, nxt // nc, nxt % nc, 1 - slot, cp, True)

    _chunk_dma(pt_ref, caches, bufs, sems, layer, b, c, slot, cp, False)
    return slot


def _chunk_operand(buf_ref, slot, cp):
    return jnp.concatenate([buf_ref[slot, j] for j in range(cp)], axis=1).astype(BF16)


def _decode_score_kernel(pt_ref, qi_ref, w_ref, kin_ref, cache_ref, keys_ref, knew_ref,
                         buf_ref, sem_ref, *, layer, cp):
    c = pl.program_id(1)
    slot = _pipeline_step(pt_ref, (cache_ref,), (buf_ref,), (sem_ref,), layer, cp)

    qi = qi_ref[0]
    w8 = w_ref[0][:, 0:1]
    dots = jnp.dot(qi, _chunk_operand(buf_ref, slot, cp), preferred_element_type=F32)
    s = jnp.sum(jnp.maximum(dots, 0.0) * w8, axis=0, keepdims=True)
    keys_ref[0, pl.ds(c, 1), :] = _sortable_key(s)

    @pl.when(c == 0)
    def _():
        ki_new = kin_ref[0][:, 0:IDX_DIM].astype(BF16).astype(F32)
        d_new = jnp.sum(qi.astype(F32) * ki_new, axis=1, keepdims=True)
        s_new = jnp.sum(jnp.maximum(d_new, 0.0) * w8, axis=0, keepdims=True)
        knew_ref[0] = jnp.broadcast_to(_sortable_key(s_new), (1, LANES))


def _decode_select_kernel(keys_ref, knew_ref, bias_ref, bnew_ref, *, topk, idx_bits, past):
    key_new = knew_ref[...][:, 0:1]
    one = lambda pred: jnp.where(pred, 1.0, 0.0)
    total = lambda fk, fn: jnp.sum(fk, axis=1, keepdims=True) + fn

    def bit_body(t, thr):
        cand = thr + (jnp.int32(1) << (31 - t))
        return jnp.where(total(one(keys_ref[...] >= cand), one(key_new >= cand)) >= topk, cand, thr)

    thr = lax.fori_loop(0, 32, bit_body, jnp.full(key_new.shape, INT_MIN, I32))
    thr = jnp.maximum(thr, INT_MIN + 1)
    need = topk - total(one(keys_ref[...] > thr), one(key_new > thr))

    idx = lax.broadcasted_iota(I32, keys_ref.shape, 1)

    def idx_body(t, x):
        cand = x + (jnp.int32(1) << (idx_bits - 1 - t))
        cnt = total(jnp.where(keys_ref[...] == thr, one(idx < cand), 0.0),
                    jnp.where(key_new == thr, one(past < cand), 0.0))
        return jnp.where(cnt <= need - 1.0, cand, x)

    x = lax.fori_loop(0, idx_bits, idx_body, jnp.zeros(key_new.shape, I32))
    keys = keys_ref[...]
    sel = jnp.where(keys == thr, one(idx <= x), one(keys > thr))
    sel_n = jnp.where(key_new == thr, one(past <= x), one(key_new > thr))
    bias_ref[...] = (1.0 - sel) * NEG_BIG
    bnew_ref[...] = jnp.broadcast_to((1.0 - sel_n) * NEG_BIG, bnew_ref.shape)


def _decode_attend_kernel(pt_ref, q_ref, bias_ref, bnew_ref, kn_ref, vn_ref, ck_ref, cv_ref, out_ref,
                          kbuf_ref, vbuf_ref, ksem_ref, vsem_ref, m_ref, l_ref, acc_ref, *, layer, cp):
    c = pl.program_id(1)
    nc = pl.num_programs(1)
    slot = _pipeline_step(pt_ref, (ck_ref, cv_ref), (kbuf_ref, vbuf_ref), (ksem_ref, vsem_ref), layer, cp)
    q = q_ref[0]

    @pl.when(c == 0)
    def _():
        k_new = kn_ref[0].astype(BF16).astype(F32)
        s_new = jnp.sum(q.astype(F32) * k_new, axis=1, keepdims=True) + bnew_ref[0][:, 0:1]
        m_ref[...] = jnp.broadcast_to(s_new, m_ref.shape)
        l_ref[...] = jnp.ones(l_ref.shape, F32)
        acc_ref[...] = jnp.broadcast_to(vn_ref[0].astype(BF16).astype(F32), acc_ref.shape)

    kt = _chunk_operand(kbuf_ref, slot, cp)
    vt = _chunk_operand(vbuf_ref, slot, cp)
    s = jnp.dot(q, kt, preferred_element_type=F32) + bias_ref[0, pl.ds(c, 1), :]
    m_old = m_ref[...][:, 0:1]
    m_new = jnp.maximum(m_old, jnp.max(s, axis=1, keepdims=True))
    alpha = jnp.exp(m_old - m_new)
    p = jnp.exp(s - m_new)
    l_ref[...] = alpha * l_ref[...] + jnp.sum(p, axis=1, keepdims=True)
    pv = lax.dot_general(p.astype(BF16), vt, (((1,), (1,)), ((), ())), preferred_element_type=F32)
    acc_ref[...] = alpha * acc_ref[...] + pv
    m_ref[...] = jnp.broadcast_to(m_new, m_ref.shape)

    @pl.when(c == nc - 1)
    def _():
        o = acc_ref[...] / l_ref[...]
        head = lax.broadcasted_iota(I32, (N_HEADS, HEAD_DIM), 0)
        out_ref[0] = jnp.where(head < KV_GROUP, o[:, 0:HEAD_DIM], o[:, HEAD_DIM:])


def _decode_attention(page_table, layer, q, qi, kiw, k_new, v_new, ckt, cvt, cikt):
    nb, n_pages = page_table.shape
    past = n_pages * PAGE_SIZE
    topk = min(TOPK_MAX, (past + 1) // 4)
    idx_bits = int(np.floor(np.log2(past))) + 1
    cp = min(DECODE_CHUNK_PAGES, n_pages)
    nc = n_pages // cp
    cpk = cp * PAGE_SIZE

    q3 = q.reshape(nb, N_KV_HEADS, KV_GROUP, HEAD_DIM)
    zq = jnp.zeros((nb, KV_GROUP, HEAD_DIM), F32)
    q_bd = jnp.concatenate([jnp.concatenate([q3[:, 0], zq], axis=2),
                            jnp.concatenate([zq, q3[:, 1]], axis=2)], axis=1).astype(BF16)
    qi3 = qi.reshape(nb, N_IDX_HEADS, IDX_DIM).astype(BF16)
    wi = kiw[:, IDX_DIM:IDX_DIM + N_IDX_HEADS] * ((IDX_DIM ** -0.5) * (N_IDX_HEADS ** -0.5))
    w8 = jnp.broadcast_to(wi[:, :, None], (nb, N_IDX_HEADS, LANES))
    kin3, kn3, vn3 = kiw[:, None, :], k_new[:, None, :], v_new[:, None, :]

    seq_blk = lambda s: pl.BlockSpec((1,) + s, lambda b, c, pt: (b,) + (0,) * len(s))
    any_spec = pl.BlockSpec(memory_space=pl.ANY)
    params = pltpu.CompilerParams(dimension_semantics=("arbitrary", "arbitrary"), vmem_limit_bytes=VMEM_LIMIT)

    keys, knew = pl.pallas_call(
        functools.partial(_decode_score_kernel, layer=layer, cp=cp),
        grid_spec=pltpu.PrefetchScalarGridSpec(
            num_scalar_prefetch=1, grid=(nb, nc),
            in_specs=[seq_blk((N_IDX_HEADS, IDX_DIM)), seq_blk((N_IDX_HEADS, LANES)), seq_blk((1, LANES)),
                      any_spec],
            out_specs=(seq_blk((nc, cpk)), seq_blk((1, LANES))),
            scratch_shapes=[pltpu.VMEM((2, cp, IDX_DIM, PAGE_SIZE), F32), pltpu.SemaphoreType.DMA((2,))]),
        out_shape=(jax.ShapeDtypeStruct((nb, nc, cpk), I32), jax.ShapeDtypeStruct((nb, 1, LANES), I32)),
        compiler_params=params, name="decode_score",
    )(page_table, qi3, w8, kin3, cikt)

    full = lambda s: pl.BlockSpec(s, lambda: (0,) * len(s))
    bias, bnew = pl.pallas_call(
        functools.partial(_decode_select_kernel, topk=topk, idx_bits=idx_bits, past=past),
        in_specs=[full((nb, past)), full((nb, LANES))],
        out_specs=(full((nb, past)), full((nb, LANES))),
        out_shape=(jax.ShapeDtypeStruct((nb, past), F32), jax.ShapeDtypeStruct((nb, LANES), F32)),
        compiler_params=pltpu.CompilerParams(vmem_limit_bytes=VMEM_LIMIT),
        name="decode_select",
    )(keys.reshape(nb, past), knew.reshape(nb, LANES))
    bias, bnew = bias.reshape(nb, nc, cpk), bnew.reshape(nb, 1, LANES)

    feat = N_KV_HEADS * HEAD_DIM
    out = pl.pallas_call(
        functools.partial(_decode_attend_kernel, layer=layer, cp=cp),
        grid_spec=pltpu.PrefetchScalarGridSpec(
            num_scalar_prefetch=1, grid=(nb, nc),
            in_specs=[seq_blk((N_HEADS, feat)), seq_blk((nc, cpk)), seq_blk((1, LANES)),
                      seq_blk((1, LANES)), seq_blk((1, LANES)), any_spec, any_spec],
            out_specs=seq_blk((N_HEADS, HEAD_DIM)),
            scratch_shapes=[pltpu.VMEM((2, cp, feat, PAGE_SIZE), F32), pltpu.VMEM((2, cp, feat, PAGE_SIZE), F32),
                            pltpu.SemaphoreType.DMA((2,)), pltpu.SemaphoreType.DMA((2,)),
                            pltpu.VMEM((N_HEADS, LANES), F32), pltpu.VMEM((N_HEADS, LANES), F32),
                            pltpu.VMEM((N_HEADS, feat), F32)]),
        out_shape=jax.ShapeDtypeStruct((nb, N_HEADS, HEAD_DIM), F32),
        compiler_params=params, name="decode_attend",
    )(page_table, q_bd, bias, bnew, kn3, vn3, ckt, cvt)
    return out.reshape(nb, N_HEADS * HEAD_DIM)


def _prep_layer(w_in, w_pool, pool_scale, w_out, w_gate, w_val, conv_w, conv_b, w_down,
                g_pre_mix, g_post_mix, g_pre_ffn, g_post_ffn):
    d = w_in.shape[0]
    sizes = (512, 128, 128, 512, IDX_DIM, N_IDX_HEADS, 512)
    cuts = np.cumsum(sizes)[:-1]
    wq, wk, wv, wqi, wki, wwi, wxp = jnp.split(w_in, [int(c) for c in cuts], axis=1)
    pad = lambda n: jnp.zeros((d, n), w_in.dtype)
    f = w_gate.shape[1]
    cw = jnp.concatenate([conv_w, conv_b[None, :], jnp.zeros((SUBLANES - CONV_WIDTH - 1, f), F32)], axis=0)
    return dict(
        wrow=jnp.concatenate([wk, wv, wki, pad(LANES - IDX_DIM), wxp], axis=1).astype(BF16),
        wt=jnp.concatenate([wq, wqi, wv, wwi, pad(BF16_ROWS - N_IDX_HEADS)], axis=1).T.astype(BF16),
        wall=jnp.concatenate([wq, wk, wv, wqi, wki, wwi, pad(LANES - IDX_DIM - N_IDX_HEADS), wxp],
                             axis=1).astype(BF16),
        wpool=w_pool.astype(BF16), pscale=pool_scale[None, :], wout=w_out.astype(BF16),
        wg=w_gate.astype(BF16), wv=w_val.astype(BF16), cw=cw, wdown=w_down.astype(BF16),
        g_pre_mix=g_pre_mix[None, :], g_post_mix=g_post_mix[None, :],
        g_pre_ffn=g_pre_ffn[None, :], g_post_ffn=g_post_ffn[None, :])


def kernel(x_prompt, x_sample, cache_k, cache_v, cache_idx_k, state_pool, state_conv, page_table,
           g_pre_mix, w_in, w_pool, pool_scale, w_out, g_post_mix, g_pre_ffn,
           w_gate, w_val, conv_w, conv_b, w_down, g_post_ffn):
    bp, sp, d = x_prompt.shape
    bs, ts, _ = x_sample.shape
    assert ts == 1 and sp % KEY_BLOCK == 0 and sp >= 2 * KEY_BLOCK
    depth, n_phys = cache_k.shape[:2]
    past = page_table.shape[1] * PAGE_SIZE
    tabs_p = _rope_tables(jnp.arange(sp))
    tabs_s = _rope_tables(jnp.full((bs,), past, I32))
    feat = N_KV_HEADS * HEAD_DIM
    ckt = jnp.transpose(cache_k, (0, 1, 3, 4, 2)).reshape(depth, n_phys, feat, PAGE_SIZE)
    cvt = jnp.transpose(cache_v, (0, 1, 3, 4, 2)).reshape(depth, n_phys, feat, PAGE_SIZE)
    cikt = jnp.transpose(cache_idx_k, (0, 1, 3, 2))

    yp = x_prompt.reshape(bp * sp, d)
    ys = x_sample.reshape(bs, d)
    outs = [[] for _ in range(10)]
    for l in range(depth):
        p = _prep_layer(w_in[l], w_pool[l], pool_scale[l], w_out[l], w_gate[l], w_val[l], conv_w[l],
                        conv_b[l], w_down[l], g_pre_mix[l], g_post_mix[l], g_pre_ffn[l], g_post_ffn[l])
        k, v, ki, xp, kb, kib, qt, qit, vt, wit = _inproj_prompt(yp, p["g_pre_mix"], p["wrow"], p["wt"], tabs_p, sp)
        a = _attn_prompt(qt, qit, wit, kb, vt, kib, bp, sp)
        x1, h2 = _mix_prompt(a, xp, yp, p["wpool"], p["pscale"], p["wout"], p["g_post_mix"], p["g_pre_ffn"], sp)
        yp, cst = _ffn_prompt(h2, x1, p["wg"], p["wv"], p["cw"], p["wdown"], p["g_post_ffn"], bp, sp)
        outs[0].append(k.reshape(bp, sp, N_KV_HEADS, HEAD_DIM))
        outs[1].append(v.reshape(bp, sp, N_KV_HEADS, HEAD_DIM))
        outs[2].append(ki.reshape(bp, sp, IDX_DIM))
        outs[3].append(xp.reshape(bp, sp, 512)[:, sp - POOL_BUF:, :])
        outs[4].append(cst[:, SUBLANES - (CONV_WIDTH - 1):, :])
        q_s, k_s, v_s, qi_s, kiw_s, xp_s = _inproj_sample(ys, p["g_pre_mix"], p["wall"], tabs_s)
        a_s = _decode_attention(page_table, l, q_s, qi_s, kiw_s, k_s, v_s, ckt, cvt, cikt)
        ext = jnp.concatenate([state_pool[l], xp_s[:, None, :]], axis=1)
        x1_s, h2_s = _mix_sample(a_s, jnp.transpose(ext, (1, 0, 2)), ys, p["wpool"], p["pscale"], p["wout"],
                                 p["g_post_mix"], p["g_pre_ffn"])
        ys, g_s = _ffn_sample(h2_s, x1_s, state_conv[l, :, 0, :], state_conv[l, :, 1, :],
                              p["wg"], p["wv"], p["cw"], p["wdown"], p["g_post_ffn"])
        outs[5].append(k_s.reshape(bs, 1, N_KV_HEADS, HEAD_DIM))
        outs[6].append(v_s.reshape(bs, 1, N_KV_HEADS, HEAD_DIM))
        outs[7].append(kiw_s[:, None, :IDX_DIM])
        outs[8].append(ext[:, 1:, :])
        outs[9].append(jnp.concatenate([state_conv[l, :, 1:, :], g_s[:, None, :]], axis=1))
    return (yp.reshape(bp, sp, d), ys.reshape(bs, 1, d)) + tuple(jnp.stack(o) for o in outs)
```

```python
import functools

import jax
import jax.numpy as jnp
import numpy as np
from jax import lax
from jax.experimental import pallas as pl
from jax.experimental.pallas import tpu as pltpu

F32 = jnp.float32
BF16 = jnp.bfloat16
I32 = jnp.int32

N_HEADS = 8
N_KV_HEADS = 2
KV_GROUP = N_HEADS // N_KV_HEADS
HEAD_DIM = 64
N_IDX_HEADS = 8
IDX_DIM = 64
ROPE_THETA = 500000.0
ROT_HALF = HEAD_DIM // 8
TOPK_MAX = 256
PAGE_SIZE = 128
POOL_WINDOWS = (2, 4, 8, 16)
POOL_GROUP_DIM = 128
POOL_BUF = max(POOL_WINDOWS) - 1
CONV_WIDTH = 3
RMS_EPS = 1e-6

LANES = 128
SUBLANES = 8
BF16_ROWS = 16
KEY_BLOCK = 128
SEQ_TILE = 512
ATTN_CHUNK = 512
DECODE_CHUNK_PAGES = 32
VMEM_LIMIT = 56 * 1024 * 1024
LOG2_E = 1.4426950408889634

INT_MIN = np.int32(-2147483648)
NEG_BIG = -1e30
UNDERFLOW_GUARD = 1e-30
FEAT = N_KV_HEADS * HEAD_DIM


def _const_spec(shape):
    nd = len(shape)
    return pl.BlockSpec(shape, lambda *_: (0,) * nd, pipeline_mode=pl.Buffered(1))


def _rms(x, g):
    ms = jnp.mean(x * x, axis=-1, keepdims=True)
    return x * lax.rsqrt(ms + RMS_EPS) * g


def _sortable_key(s):
    bits = lax.bitcast_convert_type(s, I32)
    return bits ^ ((bits >> 31) & np.int32(0x7FFFFFFF))


def _tree_sum(parts):
    parts = list(parts)
    while len(parts) > 1:
        nxt = [parts[k] + parts[k + 1] for k in range(0, len(parts) - 1, 2)]
        if len(parts) % 2:
            nxt.append(parts[-1])
        parts = nxt
    return parts[0]


def _rope_tables(pos):
    inv = ROPE_THETA ** (-jnp.arange(ROT_HALF, dtype=F32) / ROT_HALF)
    ang = pos.astype(F32)[:, None] * inv[None, :]
    cos, sin = jnp.cos(ang), jnp.sin(ang)
    p = pos.shape[0]
    one = jnp.ones((p, HEAD_DIM - 2 * ROT_HALF), F32)
    zero = jnp.zeros((p, HEAD_DIM - 2 * ROT_HALF), F32)
    z8 = jnp.zeros((p, ROT_HALF), F32)
    c64 = jnp.concatenate([cos, cos, one], axis=1)
    sa64 = jnp.concatenate([-sin, z8, zero], axis=1)
    sb64 = jnp.concatenate([z8, sin, zero], axis=1)
    rep = LANES // HEAD_DIM
    return dict(c=jnp.tile(c64, (1, rep)), sa=jnp.tile(sa64, (1, rep)), sb=jnp.tile(sb64, (1, rep)),
                cos_t=cos.T, sin_t=sin.T)


def _rope_rows(zc, c, sa, sb):
    return zc * c + pltpu.roll(zc, LANES - ROT_HALF, 1) * sa + pltpu.roll(zc, ROT_HALF, 1) * sb


def _rope_cols(zt, n_heads, cos_t, sin_t):
    pieces = []
    for h in range(n_heads):
        b = h * HEAD_DIM
        x1 = zt[b:b + ROT_HALF, :]
        x2 = zt[b + ROT_HALF:b + 2 * ROT_HALF, :]
        pieces.append(x1 * cos_t - x2 * sin_t)
        pieces.append(x2 * cos_t + x1 * sin_t)
        pieces.append(zt[b + 2 * ROT_HALF:b + HEAD_DIM, :])
    return jnp.concatenate(pieces, axis=0)


def _inproj_prompt_kernel(x_ref, g_ref, wrow_ref, wt_ref, c_ref, sa_ref, sb_ref, cos_ref, sin_ref,
                          k_ref, v_ref, ki_ref, xp_ref, kb_ref, kib_ref, qt_ref, qit_ref, vt_ref, wit_ref):
    h = _rms(x_ref[...], g_ref[...]).astype(BF16)
    z = jnp.dot(h, wrow_ref[...], preferred_element_type=F32)
    c, sa, sb = c_ref[...], sa_ref[...], sb_ref[...]
    k = _rope_rows(z[:, 0:128], c, sa, sb)
    v = z[:, 128:256]
    ki = _rope_rows(z[:, 256:384], c, sa, sb)[:, :IDX_DIM]
    k_ref[...] = k
    v_ref[...] = v
    ki_ref[...] = ki
    xp_ref[...] = z[:, 384:896]
    one_col = jnp.where(lax.broadcasted_iota(I32, k.shape, 1) == 0, 1.0, 0.0)
    kb_ref[...] = jnp.concatenate([k, one_col], axis=1).astype(BF16)
    kib_ref[...] = ki.astype(BF16)

    zt = lax.dot_general(wt_ref[...], h, (((1,), (1,)), ((), ())), preferred_element_type=F32)
    cos_t, sin_t = cos_ref[...], sin_ref[...]
    qt = _rope_cols(zt[0:512, :], N_HEADS, cos_t, sin_t) * (HEAD_DIM ** -0.5 * LOG2_E)
    qt_ref[...] = qt.astype(BF16)
    qit_ref[...] = _rope_cols(zt[512:1024, :], N_IDX_HEADS, cos_t, sin_t).astype(BF16)
    vt = zt[1024:1152, :].astype(BF16)
    ck = vt_ref.shape[2]
    for j in range(vt_ref.shape[0]):
        vt_ref[j] = vt[:, j * ck:(j + 1) * ck]
    wit_ref[...] = zt[1152:1160, :] * ((IDX_DIM ** -0.5) * (N_IDX_HEADS ** -0.5))


def _inproj_prompt(x2d, g, wrow, wt, tabs, seq):
    n, d = x2d.shape
    tm = min(SEQ_TILE, seq)
    ck = min(ATTN_CHUNK, seq)
    nt = seq // tm
    grid = (n // tm,)
    row = lambda w: pl.BlockSpec((tm, w), lambda i: (i, 0))
    tab = pl.BlockSpec((tm, LANES), lambda i: (i % nt, 0))
    tab_t = pl.BlockSpec((ROT_HALF, tm), lambda i: (0, i % nt))
    col = lambda r: pl.BlockSpec((r, tm), lambda i: (0, i))
    out_shape = (
        jax.ShapeDtypeStruct((n, 128), F32), jax.ShapeDtypeStruct((n, 128), F32),
        jax.ShapeDtypeStruct((n, IDX_DIM), F32), jax.ShapeDtypeStruct((n, 512), F32),
        jax.ShapeDtypeStruct((n, 2 * FEAT), BF16), jax.ShapeDtypeStruct((n, IDX_DIM), BF16),
        jax.ShapeDtypeStruct((512, n), BF16), jax.ShapeDtypeStruct((512, n), BF16),
        jax.ShapeDtypeStruct((n // ck, 128, ck), BF16),
        jax.ShapeDtypeStruct((N_IDX_HEADS, n), F32),
    )
    out_specs = (row(128), row(128), row(IDX_DIM), row(512), row(2 * FEAT), row(IDX_DIM),
                 col(512), col(512),
                 pl.BlockSpec((tm // ck, 128, ck), lambda i: (i, 0, 0)),
                 col(N_IDX_HEADS))
    return pl.pallas_call(
        _inproj_prompt_kernel,
        grid=grid,
        in_specs=[row(d), _const_spec((1, d)), _const_spec(wrow.shape), _const_spec(wt.shape),
                  tab, tab, tab, tab_t, tab_t],
        out_specs=out_specs, out_shape=out_shape,
        compiler_params=pltpu.CompilerParams(dimension_semantics=("arbitrary",), vmem_limit_bytes=VMEM_LIMIT),
        name="inproj_prompt",
    )(x2d, g, wrow, wt, tabs["c"], tabs["sa"], tabs["sb"], tabs["cos_t"], tabs["sin_t"])


def _attn_prompt_kernel(qt_ref, qit_ref, wit_ref, kb_ref, vt_ref, kib_ref, out_ref,
                        keys_ref, qbd_ref, qia_ref, kmax_ref, m_ref, acc_ref,
                        *, topk, idx_bits, ck):
    i = pl.program_id(1)
    nch = (i * KEY_BLOCK) // ck + 1

    @pl.when(i == 0)
    def _():
        def kmax_body(c, mx):
            off = pl.multiple_of(c * ck, ck)
            kf = kb_ref[pl.ds(off, ck), 0:FEAT].astype(F32)
            sq = kf * kf
            return tuple(jnp.maximum(mx[n], jnp.max(jnp.sum(sq[:, n * HEAD_DIM:(n + 1) * HEAD_DIM], axis=1,
                                                            keepdims=True), axis=0, keepdims=True))
                         for n in range(N_KV_HEADS))
        mx = lax.fori_loop(0, kb_ref.shape[0] // ck, kmax_body, (jnp.zeros((1, 1), F32),) * N_KV_HEADS)
        for n in range(N_KV_HEADS):
            kmax_ref[n:n + 1, :] = jnp.broadcast_to(jnp.sqrt(mx[n]), (1, KEY_BLOCK))

    zero_slab = jnp.zeros((HEAD_DIM, KEY_BLOCK), BF16)
    shift = []
    for h in range(N_HEADS):
        n = h // KV_GROUP
        cols = slice(h * KEY_BLOCK, (h + 1) * KEY_BLOCK)
        q_h = qt_ref[h * HEAD_DIM:(h + 1) * HEAD_DIM, :]
        qbd_ref[n * HEAD_DIM:(n + 1) * HEAD_DIM, cols] = q_h
        qbd_ref[(1 - n) * HEAD_DIM:(2 - n) * HEAD_DIM, cols] = zero_slab
        q_f = q_h.astype(F32)
        shift.append(-jnp.sqrt(jnp.sum(q_f * q_f, axis=0, keepdims=True)) * kmax_ref[n:n + 1, :])
    first_row = lax.broadcasted_iota(I32, (BF16_ROWS, N_HEADS * KEY_BLOCK), 0) == 0
    qbd_ref[FEAT:FEAT + BF16_ROWS, :] = jnp.where(first_row, jnp.concatenate(shift, axis=1), 0.0).astype(BF16)
    qbd_ref[FEAT + BF16_ROWS:, :] = jnp.zeros((FEAT - BF16_ROWS, N_HEADS * KEY_BLOCK), BF16)
    for h in range(N_IDX_HEADS):
        qia_ref[:, h * KEY_BLOCK:(h + 1) * KEY_BLOCK] = qit_ref[h * IDX_DIM:(h + 1) * IDX_DIM, :]

    w = wit_ref[...]
    row_iota = lax.broadcasted_iota(I32, (ck, KEY_BLOCK), 0)
    q_pos = i * KEY_BLOCK + lax.broadcasted_iota(I32, (ck, KEY_BLOCK), 1)

    def score_body(c, carry):
        off = pl.multiple_of(c * ck, ck)
        dots = jnp.dot(kib_ref[pl.ds(off, ck), :], qia_ref[...], preferred_element_type=F32)
        s = _tree_sum(jnp.maximum(dots[:, h * KEY_BLOCK:(h + 1) * KEY_BLOCK], 0.0) * w[h:h + 1, :]
                      for h in range(N_IDX_HEADS))
        keys_ref[pl.ds(off, ck), :] = jnp.where(off + row_iota <= q_pos, _sortable_key(s), INT_MIN)
        return carry

    lax.fori_loop(0, nch, score_body, 0)

    def count_where(pred):
        def body(c, acc):
            off = pl.multiple_of(c * ck, ck)
            hit = pred(keys_ref[pl.ds(off, ck), :], off)
            return acc + _tree_sum(hit[r * SUBLANES:(r + 1) * SUBLANES, :] for r in range(ck // SUBLANES))
        acc = lax.fori_loop(0, nch, body, jnp.zeros((SUBLANES, KEY_BLOCK), I32))
        return jnp.sum(acc.astype(F32), axis=0, keepdims=True)

    def bit_body(t, thr):
        cand = thr + (jnp.int32(1) << (31 - t))
        cnt = count_where(lambda blk, off: (blk >= cand).astype(I32))
        return jnp.where(cnt >= topk, cand, thr)

    thr = lax.fori_loop(0, 32, bit_body, jnp.full((1, KEY_BLOCK), INT_MIN, I32))
    thr = jnp.maximum(thr, INT_MIN + 1)
    cnt_ge = count_where(lambda blk, off: (blk >= thr).astype(I32))

    @pl.when(jnp.max(cnt_ge) > topk)
    def _():
        cnt_gt = count_where(lambda blk, off: (blk > thr).astype(I32))
        need = topk - cnt_gt

        def idx_body(t, x):
            cand = x + (jnp.int32(1) << (idx_bits - 1 - t))
            cnt = count_where(
                lambda blk, off: jnp.where(blk == thr, (off + row_iota < cand).astype(I32), 0))
            return jnp.where(cnt <= need - 1, cand, x)

        x = lax.fori_loop(0, idx_bits, idx_body, jnp.zeros((1, KEY_BLOCK), I32))

        def drop_body(c, carry):
            off = pl.multiple_of(c * ck, ck)
            blk = keys_ref[pl.ds(off, ck), :]
            keys_ref[pl.ds(off, ck), :] = jnp.where(
                blk == thr, jnp.where(off + row_iota > x, INT_MIN, blk), blk)
            return carry

        lax.fori_loop(0, nch, drop_body, 0)

    ones_rows = jnp.ones((BF16_ROWS, ck), BF16)
    acc_ref[...] = jnp.zeros(acc_ref.shape, F32)

    def fast_body(c, carry):
        off = pl.multiple_of(c * ck, ck)
        st = jnp.dot(kb_ref[pl.ds(off, ck), :], qbd_ref[...], preferred_element_type=F32)
        sel = keys_ref[pl.ds(off, ck), :] >= thr
        vt_blk = vt_ref[c]
        for n in range(N_KV_HEADS):
            p_n = jnp.concatenate(
                [jnp.where(sel, jnp.exp2(st[:, h * KEY_BLOCK:(h + 1) * KEY_BLOCK]), 0.0).astype(BF16)
                 for h in range(n * KV_GROUP, (n + 1) * KV_GROUP)], axis=1)
            lhs = jnp.concatenate([vt_blk[n * HEAD_DIM:(n + 1) * HEAD_DIM, :], ones_rows], axis=0)
            acc_ref[n] += jnp.dot(lhs, p_n, preferred_element_type=F32)
        return carry

    lax.fori_loop(0, nch, fast_body, 0)
    l_min = jnp.minimum(jnp.min(acc_ref[0, HEAD_DIM:HEAD_DIM + 1, :]), jnp.min(acc_ref[1, HEAD_DIM:HEAD_DIM + 1, :]))

    def online_body(c, carry):
        off = pl.multiple_of(c * ck, ck)
        st = jnp.dot(kb_ref[pl.ds(off, ck), :], qbd_ref[...], preferred_element_type=F32)
        sel = keys_ref[pl.ds(off, ck), :] >= thr
        vt_blk = vt_ref[c]
        m_old = m_ref[...]
        m_new = []
        for n in range(N_KV_HEADS):
            p_cols, alphas = [], []
            for g in range(KV_GROUP):
                h = n * KV_GROUP + g
                cols = slice(h * KEY_BLOCK, (h + 1) * KEY_BLOCK)
                s_c = jnp.where(sel, st[:, cols], NEG_BIG)
                m_c = jnp.maximum(m_old[:, cols], jnp.max(s_c, axis=0, keepdims=True))
                alphas.append(jnp.exp2(m_old[:, cols] - m_c))
                p_cols.append(jnp.exp2(s_c - m_c).astype(BF16))
                m_new.append(m_c)
            p_n = jnp.concatenate(p_cols, axis=1)
            a_n = jnp.concatenate(alphas, axis=1)
            lhs = jnp.concatenate([vt_blk[n * HEAD_DIM:(n + 1) * HEAD_DIM, :], ones_rows], axis=0)
            acc_ref[n] = a_n * acc_ref[n] + jnp.dot(lhs, p_n, preferred_element_type=F32)
        m_ref[...] = jnp.concatenate(m_new, axis=1)
        return carry

    @pl.when(jnp.logical_not(l_min > UNDERFLOW_GUARD))
    def _():
        m_ref[...] = jnp.full(m_ref.shape, NEG_BIG, F32)
        acc_ref[...] = jnp.zeros(acc_ref.shape, F32)
        lax.fori_loop(0, nch, online_body, 0)

    pieces = []
    for h in range(N_HEADS):
        n, g = divmod(h, KV_GROUP)
        cols = slice(g * KEY_BLOCK, (g + 1) * KEY_BLOCK)
        pieces.append(acc_ref[n, 0:HEAD_DIM, cols] / acc_ref[n, HEAD_DIM:HEAD_DIM + 1, cols])
    out_ref[...] = jnp.transpose(jnp.concatenate(pieces, axis=0)).astype(out_ref.dtype)


def _attn_prompt(qt, qit, wit, kb, vt, kib, batch, seq):
    n = batch * seq
    nq = seq // KEY_BLOCK
    ck = min(ATTN_CHUNK, seq)
    topk = min(TOPK_MAX, seq // 4)
    idx_bits = max(1, int(np.ceil(np.log2(seq))))
    qspec = lambda r: pl.BlockSpec((r, KEY_BLOCK), lambda b, i: (0, b * nq + i))
    kernel = functools.partial(_attn_prompt_kernel, topk=topk, idx_bits=idx_bits, ck=ck)
    return pl.pallas_call(
        kernel,
        grid=(batch, nq),
        in_specs=[qspec(512), qspec(512), qspec(N_IDX_HEADS),
                  pl.BlockSpec((seq, 2 * FEAT), lambda b, i: (b, 0)),
                  pl.BlockSpec((seq // ck, 128, ck), lambda b, i: (b, 0, 0)),
                  pl.BlockSpec((seq, IDX_DIM), lambda b, i: (b, 0))],
        out_specs=pl.BlockSpec((KEY_BLOCK, 512), lambda b, i: (b * nq + i, 0)),
        out_shape=jax.ShapeDtypeStruct((n, 512), BF16),
        scratch_shapes=[pltpu.VMEM((seq, KEY_BLOCK), I32),
                        pltpu.VMEM((2 * FEAT, N_HEADS * KEY_BLOCK), BF16),
                        pltpu.VMEM((IDX_DIM, N_IDX_HEADS * KEY_BLOCK), BF16),
                        pltpu.VMEM((N_KV_HEADS, KEY_BLOCK), F32),
                        pltpu.VMEM((1, N_HEADS * KEY_BLOCK), F32),
                        pltpu.VMEM((N_KV_HEADS, HEAD_DIM + BF16_ROWS, KV_GROUP * KEY_BLOCK), F32)],
        compiler_params=pltpu.CompilerParams(dimension_semantics=("arbitrary", "arbitrary"),
                                             vmem_limit_bytes=VMEM_LIMIT),
        name="attn_prompt",
    )(qt, qit, wit, kb, vt, kib)


def _mix_tail(a_bf, diff, x, wpool_ref, pscale_ref, wout_ref, gpost_ref, gpre_ref, x1_ref, h2_ref):
    outs = []
    for g in range(len(POOL_WINDOWS)):
        cols = slice(g * POOL_GROUP_DIM, (g + 1) * POOL_GROUP_DIM)
        outs.append(jnp.dot(diff[:, cols].astype(BF16), wpool_ref[g], preferred_element_type=F32))
    m = jnp.concatenate(outs, axis=1) * pscale_ref[...]
    mix_in = jnp.concatenate([a_bf, m.astype(BF16)], axis=1)
    mix = jnp.dot(mix_in, wout_ref[...], preferred_element_type=F32)
    x1 = x + _rms(mix, gpost_ref[...])
    x1_ref[...] = x1
    h2_ref[...] = _rms(x1, gpre_ref[...]).astype(BF16)


def _mix_prompt_kernel(a_ref, xp_ref, x_ref, wpool_ref, pscale_ref, wout_ref, gpost_ref, gpre_ref,
                       x1_ref, h2_ref, prev_ref, *, tm, nt):
    i = pl.program_id(0)
    halo = POOL_BUF + 1

    @pl.when(i % nt == 0)
    def _():
        prev_ref[...] = jnp.zeros(prev_ref.shape, F32)

    xp = xp_ref[...]
    ext = jnp.concatenate([prev_ref[...], xp], axis=0)
    prev_ref[...] = xp[tm - halo:, :]
    t = (i % nt) * tm + lax.broadcasted_iota(I32, (tm, 1), 0)
    diffs = []
    for g, win in enumerate(POOL_WINDOWS):
        cols = slice(g * POOL_GROUP_DIM, (g + 1) * POOL_GROUP_DIM)
        s = ext[:, cols]
        sh = 1
        while sh < win:
            s = s + pltpu.roll(s, sh, 0)
            sh *= 2
        cnt = jnp.minimum(t + 1, win).astype(F32)
        diffs.append(s[halo:, :] / cnt - xp[:, cols])
    diff = jnp.concatenate(diffs, axis=1)
    _mix_tail(a_ref[...], diff, x_ref[...], wpool_ref, pscale_ref, wout_ref, gpost_ref, gpre_ref,
              x1_ref, h2_ref)


def _mix_prompt(a, xp, x2d, wpool, pscale, wout, gpost, gpre, seq):
    n, d = x2d.shape
    tm = min(SEQ_TILE, seq)
    nt = seq // tm
    row = lambda w: pl.BlockSpec((tm, w), lambda i: (i, 0))
    return pl.pallas_call(
        functools.partial(_mix_prompt_kernel, tm=tm, nt=nt),
        grid=(n // tm,),
        in_specs=[row(512), row(512), row(d), _const_spec(wpool.shape), _const_spec((1, 512)),
                  _const_spec(wout.shape), _const_spec((1, d)), _const_spec((1, d))],
        out_specs=(row(d), row(d)),
        out_shape=(jax.ShapeDtypeStruct((n, d), F32), jax.ShapeDtypeStruct((n, d), BF16)),
        scratch_shapes=[pltpu.VMEM((POOL_BUF + 1, 512), F32)],
        compiler_params=pltpu.CompilerParams(dimension_semantics=("arbitrary",), vmem_limit_bytes=VMEM_LIMIT),
        name="mix_prompt",
    )(a, xp, x2d, wpool, pscale, wout, gpost, gpre)


def _mix_sample_kernel(a_ref, ext_ref, x_ref, wpool_ref, pscale_ref, wout_ref, gpost_ref, gpre_ref,
                       x1_ref, h2_ref):
    rows = POOL_BUF + 1
    last = ext_ref[rows - 1]
    diffs = []
    for g, win in enumerate(POOL_WINDOWS):
        cols = slice(g * POOL_GROUP_DIM, (g + 1) * POOL_GROUP_DIM)
        wsum = last[:, cols]
        for r in range(rows - win, rows - 1):
            wsum = wsum + ext_ref[r][:, cols]
        diffs.append(wsum / float(win) - last[:, cols])
    diff = jnp.concatenate(diffs, axis=1)
    _mix_tail(a_ref[...].astype(BF16), diff, x_ref[...], wpool_ref, pscale_ref, wout_ref, gpost_ref,
              gpre_ref, x1_ref, h2_ref)


def _mix_sample(a, ext, x2d, wpool, pscale, wout, gpost, gpre):
    n, d = x2d.shape
    full = lambda s: pl.BlockSpec(s, lambda: (0,) * len(s))
    return pl.pallas_call(
        _mix_sample_kernel,
        in_specs=[full(a.shape), full(ext.shape), full(x2d.shape), full(wpool.shape), full((1, 512)),
                  full(wout.shape), full((1, d)), full((1, d))],
        out_specs=(full((n, d)), full((n, d))),
        out_shape=(jax.ShapeDtypeStruct((n, d), F32), jax.ShapeDtypeStruct((n, d), BF16)),
        compiler_params=pltpu.CompilerParams(vmem_limit_bytes=VMEM_LIMIT),
        name="mix_sample",
    )(a, ext, x2d, wpool, pscale, wout, gpost, gpre)


def _ffn_tail(c, val, x1, wdown_ref, gpost_ref, x2_ref):
    y = (jax.nn.gelu(c, approximate=True) * val).astype(BF16)
    f = jnp.dot(y, wdown_ref[...], preferred_element_type=F32)
    x2_ref[...] = x1 + _rms(f, gpost_ref[...])


def _ffn_prompt_kernel(h2_ref, x1_ref, wg_ref, wv_ref, cw_ref, wdown_ref, gpost_ref,
                       x2_ref, cst_ref, prev_ref, *, tm, nt):
    i = pl.program_id(0)

    @pl.when(i % nt == 0)
    def _():
        prev_ref[...] = jnp.zeros(prev_ref.shape, F32)

    h2 = h2_ref[...]
    g = jnp.dot(h2, wg_ref[...], preferred_element_type=F32)
    val = jnp.dot(h2, wv_ref[...], preferred_element_type=F32)
    ext = jnp.concatenate([prev_ref[...], g], axis=0)
    tail = g[tm - SUBLANES:, :]
    prev_ref[...] = tail
    cst_ref[0] = tail
    cw = cw_ref[...]
    g1 = pltpu.roll(ext, 1, 0)[SUBLANES:, :]
    g2 = pltpu.roll(ext, 2, 0)[SUBLANES:, :]
    c = cw[3:4, :] + g2 * cw[0:1, :] + g1 * cw[1:2, :] + g * cw[2:3, :]
    _ffn_tail(c, val, x1_ref[...], wdown_ref, gpost_ref, x2_ref)


def _ffn_prompt(h2, x1, wg, wv, cw, wdown, gpost, batch, seq):
    n, d = x1.shape
    f = wg.shape[1]
    tm = min(256, seq)
    nt = seq // tm
    row = lambda w: pl.BlockSpec((tm, w), lambda i: (i, 0))
    return pl.pallas_call(
        functools.partial(_ffn_prompt_kernel, tm=tm, nt=nt),
        grid=(n // tm,),
        in_specs=[row(d), row(d), _const_spec(wg.shape), _const_spec(wv.shape), _const_spec(cw.shape),
                  _const_spec(wdown.shape), _const_spec((1, d))],
        out_specs=(row(d), pl.BlockSpec((1, SUBLANES, f), lambda i: (i // nt, 0, 0))),
        out_shape=(jax.ShapeDtypeStruct((n, d), F32), jax.ShapeDtypeStruct((batch, SUBLANES, f), F32)),
        scratch_shapes=[pltpu.VMEM((SUBLANES, f), F32)],
        compiler_params=pltpu.CompilerParams(dimension_semantics=("arbitrary",), vmem_limit_bytes=VMEM_LIMIT),
        name="ffn_prompt",
    )(h2, x1, wg, wv, cw, wdown, gpost)


def _ffn_sample_kernel(h2_ref, x1_ref, b0_ref, b1_ref, wg_ref, wv_ref, cw_ref, wdown_ref, gpost_ref,
                       x2_ref, g_ref):
    h2 = h2_ref[...]
    g = jnp.dot(h2, wg_ref[...], preferred_element_type=F32)
    val = jnp.dot(h2, wv_ref[...], preferred_element_type=F32)
    g_ref[...] = g
    cw = cw_ref[...]
    c = cw[3:4, :] + b0_ref[...] * cw[0:1, :] + b1_ref[...] * cw[1:2, :] + g * cw[2:3, :]
    _ffn_tail(c, val, x1_ref[...], wdown_ref, gpost_ref, x2_ref)


def _ffn_sample(h2, x1, b0, b1, wg, wv, cw, wdown, gpost):
    n, d = x1.shape
    f = wg.shape[1]
    full = lambda s: pl.BlockSpec(s, lambda: (0,) * len(s))
    args = (h2, x1, b0, b1, wg, wv, cw, wdown, gpost)
    return pl.pallas_call(
        _ffn_sample_kernel,
        in_specs=[full(a.shape) for a in args],
        out_specs=(full((n, d)), full((n, f))),
        out_shape=(jax.ShapeDtypeStruct((n, d), F32), jax.ShapeDtypeStruct((n, f), F32)),
        compiler_params=pltpu.CompilerParams(vmem_limit_bytes=VMEM_LIMIT),
        name="ffn_sample",
    )(*args)


def _inproj_sample_kernel(x_ref, g_ref, w_ref, c_ref, sa_ref, sb_ref,
                          q_ref, k_ref, v_ref, qi_ref, kiw_ref, xp_ref):
    h = _rms(x_ref[...], g_ref[...]).astype(BF16)
    z = jnp.dot(h, w_ref[...], preferred_element_type=F32)
    c, sa, sb = c_ref[...], sa_ref[...], sb_ref[...]
    rope = lambda lo: _rope_rows(z[:, lo:lo + LANES], c, sa, sb)
    q_ref[...] = jnp.concatenate([rope(j * LANES) for j in range(4)], axis=1) * (HEAD_DIM ** -0.5)
    k_ref[...] = rope(512)
    v_ref[...] = z[:, 640:768]
    qi_ref[...] = jnp.concatenate([rope(768 + j * LANES) for j in range(4)], axis=1)
    lane = lax.broadcasted_iota(I32, (1, LANES), 1)
    kiw = z[:, 1280:1408]
    kiw_ref[...] = jnp.where(lane < IDX_DIM, _rope_rows(kiw, c, sa, sb), kiw)
    xp_ref[...] = z[:, 1408:1920]


def _inproj_sample(x2d, g, wall, tabs):
    n, d = x2d.shape
    full = lambda s: pl.BlockSpec(s, lambda: (0,) * len(s))
    args = (x2d, g, wall, tabs["c"], tabs["sa"], tabs["sb"])
    widths = (512, 128, 128, 512, 128, 512)
    return pl.pallas_call(
        _inproj_sample_kernel,
        in_specs=[full(a.shape) for a in args],
        out_specs=tuple(full((n, w)) for w in widths),
        out_shape=tuple(jax.ShapeDtypeStruct((n, w), F32) for w in widths),
        compiler_params=pltpu.CompilerParams(vmem_limit_bytes=VMEM_LIMIT),
        name="inproj_sample",
    )(*args)


def _page_copy(cache_ref, buf_ref, sem_ref, layer, page, slot, j):
    return pltpu.make_async_copy(cache_ref.at[layer, page], buf_ref.at[slot, j], sem_ref.at[slot])


def _chunk_dma(pt_ref, caches, bufs, sems, layer, b, c, slot, cp, start):
    for j in range(cp):
        page = pt_ref[b, c * cp + j]
        for cache_ref, buf_ref, sem_ref in zip(caches, bufs, sems):
            desc = _page_copy(cache_ref, buf_ref, sem_ref, layer, page, slot, j)
            if start:
                desc.start()
            else:
                desc.wait()


def _pipeline_step(pt_ref, caches, bufs, sems, layer, cp):
    b, c = pl.program_id(0), pl.program_id(1)
    nb, nc = pl.num_programs(0), pl.num_programs(1)
    step = b * nc + c
    slot = step % 2

    @pl.when(step == 0)
    def _():
        _chunk_dma(pt_ref, caches, bufs, sems, layer, b, c, slot, cp, True)

    @pl.when(step + 1 < nb * nc)
    def _():
        nxt = step + 1
        _chunk_dma(pt_ref, caches, bufs, sems, layer, nxt // nc, nxt % nc, 1 - slot, cp, True)

    _chunk_dma(pt_ref, caches, bufs, sems, layer, b, c, slot, cp, False)
    return slot


def _chunk_operand(buf_ref, slot, cp):
    return jnp.concatenate([buf_ref[slot, j] for j in range(cp)], axis=1).astype(BF16)


def _decode_score_kernel(pt_ref, qi_ref, w_ref, kin_ref, cache_ref, keys_ref, knew_ref,
                         buf_ref, sem_ref, *, layer, cp):
    c = pl.program_id(1)
    slot = _pipeline_step(pt_ref, (cache_ref,), (buf_ref,), (sem_ref,), layer, cp)

    qi = qi_ref[0]
    w8 = w_ref[0][:, 0:1]
    dots = jnp.dot(qi, _chunk_operand(buf_ref, slot, cp), preferred_element_type=F32)
    s = jnp.sum(jnp.maximum(dots, 0.0) * w8, axis=0, keepdims=True)
    keys_ref[0, pl.ds(c, 1), :] = _sortable_key(s)

    @pl.when(c == 0)
    def _():
        ki_new = kin_ref[0][:, 0:IDX_DIM].astype(BF16).astype(F32)
        d_new = jnp.sum(qi.astype(F32) * ki_new, axis=1, keepdims=True)
        s_new = jnp.sum(jnp.maximum(d_new, 0.0) * w8, axis=0, keepdims=True)
        knew_ref[0] = jnp.broadcast_to(_sortable_key(s_new), (1, LANES))


def _decode_select_kernel(keys_ref, knew_ref, bias_ref, bnew_ref, *, topk, idx_bits, past):
    key_new = knew_ref[...][:, 0:1]
    one = lambda pred: jnp.where(pred, 1.0, 0.0)
    total = lambda fk, fn: jnp.sum(fk, axis=1, keepdims=True) + fn

    def bit_body(t, thr):
        cand = thr + (jnp.int32(1) << (31 - t))
        return jnp.where(total(one(keys_ref[...] >= cand), one(key_new >= cand)) >= topk, cand, thr)

    thr = lax.fori_loop(0, 32, bit_body, jnp.full(key_new.shape, INT_MIN, I32))
    thr = jnp.maximum(thr, INT_MIN + 1)
    need = topk - total(one(keys_ref[...] > thr), one(key_new > thr))

    idx = lax.broadcasted_iota(I32, keys_ref.shape, 1)

    def idx_body(t, x):
        cand = x + (jnp.int32(1) << (idx_bits - 1 - t))
        cnt = total(jnp.where(keys_ref[...] == thr, one(idx < cand), 0.0),
                    jnp.where(key_new == thr, one(past < cand), 0.0))
        return jnp.where(cnt <= need - 1.0, cand, x)

    x = lax.fori_loop(0, idx_bits, idx_body, jnp.zeros(key_new.shape, I32))
    keys = keys_ref[...]
    sel = jnp.where(keys == thr, one(idx <= x), one(keys > thr))
    sel_n = jnp.where(key_new == thr, one(past <= x), one(key_new > thr))
    bias_ref[...] = (1.0 - sel) * NEG_BIG
    bnew_ref[...] = jnp.broadcast_to((1.0 - sel_n) * NEG_BIG, bnew_ref.shape)


def _decode_attend_kernel(pt_ref, q_ref, bias_ref, bnew_ref, kn_ref, vn_ref, ck_ref, cv_ref, out_ref,
                          kbuf_ref, vbuf_ref, ksem_ref, vsem_ref, m_ref, l_ref, acc_ref, *, layer, cp):
    c = pl.program_id(1)
    nc = pl.num_programs(1)
    slot = _pipeline_step(pt_ref, (ck_ref, cv_ref), (kbuf_ref, vbuf_ref), (ksem_ref, vsem_ref), layer, cp)
    q = q_ref[0]

    @pl.when(c == 0)
    def _():
        k_new = kn_ref[0].astype(BF16).astype(F32)
        s_new = jnp.sum(q.astype(F32) * k_new, axis=1, keepdims=True) + bnew_ref[0][:, 0:1]
        m_ref[...] = jnp.broadcast_to(s_new, m_ref.shape)
        l_ref[...] = jnp.ones(l_ref.shape, F32)
        acc_ref[...] = jnp.broadcast_to(vn_ref[0].astype(BF16).astype(F32), acc_ref.shape)

    kt = _chunk_operand(kbuf_ref, slot, cp)
    vt = _chunk_operand(vbuf_ref, slot, cp)
    s = jnp.dot(q, kt, preferred_element_type=F32) + bias_ref[0, pl.ds(c, 1), :]
    m_old = m_ref[...][:, 0:1]
    m_new = jnp.maximum(m_old, jnp.max(s, axis=1, keepdims=True))
    alpha = jnp.exp(m_old - m_new)
    p = jnp.exp(s - m_new)
    l_ref[...] = alpha * l_ref[...] + jnp.sum(p, axis=1, keepdims=True)
    pv = lax.dot_general(p.astype(BF16), vt, (((1,), (1,)), ((), ())), preferred_element_type=F32)
    acc_ref[...] = alpha * acc_ref[...] + pv
    m_ref[...] = jnp.broadcast_to(m_new, m_ref.shape)

    @pl.when(c == nc - 1)
    def _():
        o = acc_ref[...] / l_ref[...]
        head = lax.broadcasted_iota(I32, (N_HEADS, HEAD_DIM), 0)
        out_ref[0] = jnp.where(head < KV_GROUP, o[:, 0:HEAD_DIM], o[:, HEAD_DIM:])


def _decode_attention(page_table, layer, q, qi, kiw, k_new, v_new, ckt, cvt, cikt):
    nb, n_pages = page_table.shape
    past = n_pages * PAGE_SIZE
    topk = min(TOPK_MAX, (past + 1) // 4)
    idx_bits = int(np.floor(np.log2(past))) + 1
    cp = min(DECODE_CHUNK_PAGES, n_pages)
    nc = n_pages // cp
    cpk = cp * PAGE_SIZE

    q3 = q.reshape(nb, N_KV_HEADS, KV_GROUP, HEAD_DIM)
    zq = jnp.zeros((nb, KV_GROUP, HEAD_DIM), F32)
    q_bd = jnp.concatenate([jnp.concatenate([q3[:, 0], zq], axis=2),
                            jnp.concatenate([zq, q3[:, 1]], axis=2)], axis=1).astype(BF16)
    qi3 = qi.reshape(nb, N_IDX_HEADS, IDX_DIM).astype(BF16)
    wi = kiw[:, IDX_DIM:IDX_DIM + N_IDX_HEADS] * ((IDX_DIM ** -0.5) * (N_IDX_HEADS ** -0.5))
    w8 = jnp.broadcast_to(wi[:, :, None], (nb, N_IDX_HEADS, LANES))
    kin3, kn3, vn3 = kiw[:, None, :], k_new[:, None, :], v_new[:, None, :]

    seq_blk = lambda s: pl.BlockSpec((1,) + s, lambda b, c, pt: (b,) + (0,) * len(s))
    any_spec = pl.BlockSpec(memory_space=pl.ANY)
    params = pltpu.CompilerParams(dimension_semantics=("arbitrary", "arbitrary"), vmem_limit_bytes=VMEM_LIMIT)

    keys, knew = pl.pallas_call(
        functools.partial(_decode_score_kernel, layer=layer, cp=cp),
        grid_spec=pltpu.PrefetchScalarGridSpec(
            num_scalar_prefetch=1, grid=(nb, nc),
            in_specs=[seq_blk((N_IDX_HEADS, IDX_DIM)), seq_blk((N_IDX_HEADS, LANES)), seq_blk((1, LANES)),
                      any_spec],
            out_specs=(seq_blk((nc, cpk)), seq_blk((1, LANES))),
            scratch_shapes=[pltpu.VMEM((2, cp, IDX_DIM, PAGE_SIZE), F32), pltpu.SemaphoreType.DMA((2,))]),
        out_shape=(jax.ShapeDtypeStruct((nb, nc, cpk), I32), jax.ShapeDtypeStruct((nb, 1, LANES), I32)),
        compiler_params=params, name="decode_score",
    )(page_table, qi3, w8, kin3, cikt)

    full = lambda s: pl.BlockSpec(s, lambda: (0,) * len(s))
    bias, bnew = pl.pallas_call(
        functools.partial(_decode_select_kernel, topk=topk, idx_bits=idx_bits, past=past),
        in_specs=[full((nb, past)), full((nb, LANES))],
        out_specs=(full((nb, past)), full((nb, LANES))),
        out_shape=(jax.ShapeDtypeStruct((nb, past), F32), jax.ShapeDtypeStruct((nb, LANES), F32)),
        compiler_params=pltpu.CompilerParams(vmem_limit_bytes=VMEM_LIMIT),
        name="decode_select",
    )(keys.reshape(nb, past), knew.reshape(nb, LANES))
    bias, bnew = bias.reshape(nb, nc, cpk), bnew.reshape(nb, 1, LANES)

    feat = N_KV_HEADS * HEAD_DIM
    out = pl.pallas_call(
        functools.partial(_decode_attend_kernel, layer=layer, cp=cp),
        grid_spec=pltpu.PrefetchScalarGridSpec(
            num_scalar_prefetch=1, grid=(nb, nc),
            in_specs=[seq_blk((N_HEADS, feat)), seq_blk((nc, cpk)), seq_blk((1, LANES)),
                      seq_blk((1, LANES)), seq_blk((1, LANES)), any_spec, any_spec],
            out_specs=seq_blk((N_HEADS, HEAD_DIM)),
            scratch_shapes=[pltpu.VMEM((2, cp, feat, PAGE_SIZE), F32), pltpu.VMEM((2, cp, feat, PAGE_SIZE), F32),
                            pltpu.SemaphoreType.DMA((2,)), pltpu.SemaphoreType.DMA((2,)),
                            pltpu.VMEM((N_HEADS, LANES), F32), pltpu.VMEM((N_HEADS, LANES), F32),
                            pltpu.VMEM((N_HEADS, feat), F32)]),
        out_shape=jax.ShapeDtypeStruct((nb, N_HEADS, HEAD_DIM), F32),
        compiler_params=params, name="decode_attend",
    )(page_table, q_bd, bias, bnew, kn3, vn3, ckt, cvt)
    return out.reshape(nb, N_HEADS * HEAD_DIM)


def _prep_layer(w_in, w_pool, pool_scale, w_out, w_gate, w_val, conv_w, conv_b, w_down,
                g_pre_mix, g_post_mix, g_pre_ffn, g_post_ffn):
    d = w_in.shape[0]
    sizes = (512, 128, 128, 512, IDX_DIM, N_IDX_HEADS, 512)
    cuts = np.cumsum(sizes)[:-1]
    wq, wk, wv, wqi, wki, wwi, wxp = jnp.split(w_in, [int(c) for c in cuts], axis=1)
    pad = lambda n: jnp.zeros((d, n), w_in.dtype)
    f = w_gate.shape[1]
    cw = jnp.concatenate([conv_w, conv_b[None, :], jnp.zeros((SUBLANES - CONV_WIDTH - 1, f), F32)], axis=0)
    return dict(
        wrow=jnp.concatenate([wk, wv, wki, pad(LANES - IDX_DIM), wxp], axis=1).astype(BF16),
        wt=jnp.concatenate([wq, wqi, wv, wwi, pad(BF16_ROWS - N_IDX_HEADS)], axis=1).T.astype(BF16),
        wall=jnp.concatenate([wq, wk, wv, wqi, wki, wwi, pad(LANES - IDX_DIM - N_IDX_HEADS), wxp],
                             axis=1).astype(BF16),
        wpool=w_pool.astype(BF16), pscale=pool_scale[None, :], wout=w_out.astype(BF16),
        wg=w_gate.astype(BF16), wv=w_val.astype(BF16), cw=cw, wdown=w_down.astype(BF16),
        g_pre_mix=g_pre_mix[None, :], g_post_mix=g_post_mix[None, :],
        g_pre_ffn=g_pre_ffn[None, :], g_post_ffn=g_post_ffn[None, :])


def kernel(x_prompt, x_sample, cache_k, cache_v, cache_idx_k, state_pool, state_conv, page_table,
           g_pre_mix, w_in, w_pool, pool_scale, w_out, g_post_mix, g_pre_ffn,
           w_gate, w_val, conv_w, conv_b, w_down, g_post_ffn):
    bp, sp, d = x_prompt.shape
    bs, ts, _ = x_sample.shape
    assert ts == 1 and sp % KEY_BLOCK == 0 and sp >= 2 * KEY_BLOCK
    depth, n_phys = cache_k.shape[:2]
    past = page_table.shape[1] * PAGE_SIZE
    tabs_p = _rope_tables(jnp.arange(sp))
    tabs_s = _rope_tables(jnp.full((bs,), past, I32))
    feat = N_KV_HEADS * HEAD_DIM
    ckt = jnp.transpose(cache_k, (0, 1, 3, 4, 2)).reshape(depth, n_phys, feat, PAGE_SIZE)
    cvt = jnp.transpose(cache_v, (0, 1, 3, 4, 2)).reshape(depth, n_phys, feat, PAGE_SIZE)
    cikt = jnp.transpose(cache_idx_k, (0, 1, 3, 2))

    yp = x_prompt.reshape(bp * sp, d)
    ys = x_sample.reshape(bs, d)
    outs = [[] for _ in range(10)]
    for l in range(depth):
        p = _prep_layer(w_in[l], w_pool[l], pool_scale[l], w_out[l], w_gate[l], w_val[l], conv_w[l],
                        conv_b[l], w_down[l], g_pre_mix[l], g_post_mix[l], g_pre_ffn[l], g_post_ffn[l])
        k, v, ki, xp, kb, kib, qt, qit, vt, wit = _inproj_prompt(yp, p["g_pre_mix"], p["wrow"], p["wt"], tabs_p, sp)
        a = _attn_prompt(qt, qit, wit, kb, vt, kib, bp, sp)
        x1, h2 = _mix_prompt(a, xp, yp, p["wpool"], p["pscale"], p["wout"], p["g_post_mix"], p["g_pre_ffn"], sp)
        yp, cst = _ffn_prompt(h2, x1, p["wg"], p["wv"], p["cw"], p["wdown"], p["g_post_ffn"], bp, sp)
        outs[0].append(k.reshape(bp, sp, N_KV_HEADS, HEAD_DIM))
        outs[1].append(v.reshape(bp, sp, N_KV_HEADS, HEAD_DIM))
        outs[2].append(ki.reshape(bp, sp, IDX_DIM))
        outs[3].append(xp.reshape(bp, sp, 512)[:, sp - POOL_BUF:, :])
        outs[4].append(cst[:, SUBLANES - (CONV_WIDTH - 1):, :])
        q_s, k_s, v_s, qi_s, kiw_s, xp_s = _inproj_sample(ys, p["g_pre_mix"], p["wall"], tabs_s)
        a_s = _decode_attention(page_table, l, q_s, qi_s, kiw_s, k_s, v_s, ckt, cvt, cikt)
        ext = jnp.concatenate([state_pool[l], xp_s[:, None, :]], axis=1)
        x1_s, h2_s = _mix_sample(a_s, jnp.transpose(ext, (1, 0, 2)), ys, p["wpool"], p["pscale"], p["wout"],
                                 p["g_post_mix"], p["g_pre_ffn"])
        ys, g_s = _ffn_sample(h2_s, x1_s, state_conv[l, :, 0, :], state_conv[l, :, 1, :],
                              p["wg"], p["wv"], p["cw"], p["wdown"], p["g_post_ffn"])
        outs[5].append(k_s.reshape(bs, 1, N_KV_HEADS, HEAD_DIM))
        outs[6].append(v_s.reshape(bs, 1, N_KV_HEADS, HEAD_DIM))
        outs[7].append(kiw_s[:, None, :IDX_DIM])
        outs[8].append(ext[:, 1:, :])
        outs[9].append(jnp.concatenate([state_conv[l, :, 1:, :], g_s[:, None, :]], axis=1))
    return (yp.reshape(bp, sp, d), ys.reshape(bs, 1, d)) + tuple(jnp.stack(o) for o in outs)
```

```python
import functools

import jax
import jax.numpy as jnp
import numpy as np
from jax import lax
from jax.experimental import pallas as pl
from jax.experimental.pallas import tpu as pltpu

F32 = jnp.float32
BF16 = jnp.bfloat16
I32 = jnp.int32

N_HEADS = 8
N_KV_HEADS = 2
KV_GROUP = N_HEADS // N_KV_HEADS
HEAD_DIM = 64
N_IDX_HEADS = 8
IDX_DIM = 64
ROPE_THETA = 500000.0
ROT_HALF = HEAD_DIM // 8
TOPK_MAX = 256
PAGE_SIZE = 128
POOL_WINDOWS = (2, 4, 8, 16)
POOL_GROUP_DIM = 128
POOL_BUF = max(POOL_WINDOWS) - 1
CONV_WIDTH = 3
RMS_EPS = 1e-6

LANES = 128
SUBLANES = 8
BF16_ROWS = 16
KEY_BLOCK = 256
SEQ_TILE = 512
ATTN_CHUNK = 512
LOW_BITS = 5
DECODE_CHUNK_PAGES = 32
VMEM_LIMIT = 56 * 1024 * 1024
LOG2_E = 1.4426950408889634

INT_MIN = np.int32(-2147483648)
NEG_BIG = -1e30
UNDERFLOW_GUARD = 1e-30
FEAT = N_KV_HEADS * HEAD_DIM


def _const_spec(shape):
    nd = len(shape)
    return pl.BlockSpec(shape, lambda *_: (0,) * nd, pipeline_mode=pl.Buffered(1))


def _rms(x, g):
    ms = jnp.mean(x * x, axis=-1, keepdims=True)
    return x * lax.rsqrt(ms + RMS_EPS) * g


def _sortable_key(s):
    bits = lax.bitcast_convert_type(s, I32)
    return bits ^ ((bits >> 31) & np.int32(0x7FFFFFFF))


def _tree_sum(parts):
    parts = list(parts)
    while len(parts) > 1:
        nxt = [parts[k] + parts[k + 1] for k in range(0, len(parts) - 1, 2)]
        if len(parts) % 2:
            nxt.append(parts[-1])
        parts = nxt
    return parts[0]


def _rope_tables(pos):
    inv = ROPE_THETA ** (-jnp.arange(ROT_HALF, dtype=F32) / ROT_HALF)
    ang = pos.astype(F32)[:, None] * inv[None, :]
    cos, sin = jnp.cos(ang), jnp.sin(ang)
    p = pos.shape[0]
    one = jnp.ones((p, HEAD_DIM - 2 * ROT_HALF), F32)
    zero = jnp.zeros((p, HEAD_DIM - 2 * ROT_HALF), F32)
    z8 = jnp.zeros((p, ROT_HALF), F32)
    c64 = jnp.concatenate([cos, cos, one], axis=1)
    sa64 = jnp.concatenate([-sin, z8, zero], axis=1)
    sb64 = jnp.concatenate([z8, sin, zero], axis=1)
    rep = LANES // HEAD_DIM
    return dict(c=jnp.tile(c64, (1, rep)), sa=jnp.tile(sa64, (1, rep)), sb=jnp.tile(sb64, (1, rep)),
                cos_t=cos.T, sin_t=sin.T)


def _rope_rows(zc, c, sa, sb):
    return zc * c + pltpu.roll(zc, LANES - ROT_HALF, 1) * sa + pltpu.roll(zc, ROT_HALF, 1) * sb


def _rope_cols(zt, n_heads, cos_t, sin_t):
    pieces = []
    for h in range(n_heads):
        b = h * HEAD_DIM
        x1 = zt[b:b + ROT_HALF, :]
        x2 = zt[b + ROT_HALF:b + 2 * ROT_HALF, :]
        pieces.append(x1 * cos_t - x2 * sin_t)
        pieces.append(x2 * cos_t + x1 * sin_t)
        pieces.append(zt[b + 2 * ROT_HALF:b + HEAD_DIM, :])
    return jnp.concatenate(pieces, axis=0)


def _inproj_prompt_kernel(x_ref, g_ref, wrow_ref, wt_ref, c_ref, sa_ref, sb_ref, cos_ref, sin_ref,
                          k_ref, v_ref, ki_ref, xp_ref, kb_ref, kib_ref, qt_ref, qit_ref, vt_ref, wit_ref):
    h = _rms(x_ref[...], g_ref[...]).astype(BF16)
    z = jnp.dot(h, wrow_ref[...], preferred_element_type=F32)
    c, sa, sb = c_ref[...], sa_ref[...], sb_ref[...]
    k = _rope_rows(z[:, 0:128], c, sa, sb)
    v = z[:, 128:256]
    ki = _rope_rows(z[:, 256:384], c, sa, sb)[:, :IDX_DIM]
    k_ref[...] = k
    v_ref[...] = v
    ki_ref[...] = ki
    xp_ref[...] = z[:, 384:896]
    one_col = jnp.where(lax.broadcasted_iota(I32, k.shape, 1) == 0, 1.0, 0.0)
    kb_ref[...] = jnp.concatenate([k, one_col], axis=1).astype(BF16)
    kib_ref[...] = ki.astype(BF16)

    zt = lax.dot_general(wt_ref[...], h, (((1,), (1,)), ((), ())), preferred_element_type=F32)
    cos_t, sin_t = cos_ref[...], sin_ref[...]
    qt = _rope_cols(zt[0:512, :], N_HEADS, cos_t, sin_t) * (HEAD_DIM ** -0.5 * LOG2_E)
    qt_ref[...] = qt.astype(BF16)
    qit_ref[...] = _rope_cols(zt[512:1024, :], N_IDX_HEADS, cos_t, sin_t).astype(BF16)
    vt = zt[1024:1152, :].astype(BF16)
    ck = vt_ref.shape[2]
    for j in range(vt_ref.shape[0]):
        vt_ref[j] = vt[:, j * ck:(j + 1) * ck]
    wit_ref[...] = zt[1152:1160, :] * ((IDX_DIM ** -0.5) * (N_IDX_HEADS ** -0.5))


def _inproj_prompt(x2d, g, wrow, wt, tabs, seq):
    n, d = x2d.shape
    tm = min(SEQ_TILE, seq)
    ck = min(ATTN_CHUNK, seq)
    nt = seq // tm
    grid = (n // tm,)
    row = lambda w: pl.BlockSpec((tm, w), lambda i: (i, 0))
    tab = pl.BlockSpec((tm, LANES), lambda i: (i % nt, 0))
    tab_t = pl.BlockSpec((ROT_HALF, tm), lambda i: (0, i % nt))
    col = lambda r: pl.BlockSpec((r, tm), lambda i: (0, i))
    out_shape = (
        jax.ShapeDtypeStruct((n, 128), F32), jax.ShapeDtypeStruct((n, 128), F32),
        jax.ShapeDtypeStruct((n, IDX_DIM), F32), jax.ShapeDtypeStruct((n, 512), F32),
        jax.ShapeDtypeStruct((n, 2 * FEAT), BF16), jax.ShapeDtypeStruct((n, IDX_DIM), BF16),
        jax.ShapeDtypeStruct((512, n), BF16), jax.ShapeDtypeStruct((512, n), BF16),
        jax.ShapeDtypeStruct((n // ck, 128, ck), BF16),
        jax.ShapeDtypeStruct((N_IDX_HEADS, n), F32),
    )
    out_specs = (row(128), row(128), row(IDX_DIM), row(512), row(2 * FEAT), row(IDX_DIM),
                 col(512), col(512),
                 pl.BlockSpec((tm // ck, 128, ck), lambda i: (i, 0, 0)),
                 col(N_IDX_HEADS))
    return pl.pallas_call(
        _inproj_prompt_kernel,
        grid=grid,
        in_specs=[row(d), _const_spec((1, d)), _const_spec(wrow.shape), _const_spec(wt.shape),
                  tab, tab, tab, tab_t, tab_t],
        out_specs=out_specs, out_shape=out_shape,
        compiler_params=pltpu.CompilerParams(dimension_semantics=("arbitrary",), vmem_limit_bytes=VMEM_LIMIT),
        name="inproj_prompt",
    )(x2d, g, wrow, wt, tabs["c"], tabs["sa"], tabs["sb"], tabs["cos_t"], tabs["sin_t"])


def _attn_prompt_kernel(qt_ref, qit_ref, wit_ref, kb_ref, vt_ref, kib_ref, out_ref,
                        keys_ref, thr_ref, cnt_ref, qbd_ref, qia_ref, kmax_ref, m_ref, acc_ref,
                        *, topk, idx_bits, ck):
    i = pl.program_id(1)
    nch = (i * KEY_BLOCK) // ck + 1

    @pl.when(i == 0)
    def _():
        def kmax_body(c, mx):
            off = pl.multiple_of(c * ck, ck)
            kf = kb_ref[pl.ds(off, ck), 0:FEAT].astype(F32)
            sq = kf * kf
            return tuple(jnp.maximum(mx[n], jnp.max(jnp.sum(sq[:, n * HEAD_DIM:(n + 1) * HEAD_DIM], axis=1,
                                                            keepdims=True), axis=0, keepdims=True))
                         for n in range(N_KV_HEADS))
        mx = lax.fori_loop(0, kb_ref.shape[0] // ck, kmax_body, (jnp.zeros((1, 1), F32),) * N_KV_HEADS)
        for n in range(N_KV_HEADS):
            kmax_ref[n:n + 1, :] = jnp.broadcast_to(jnp.sqrt(mx[n]), (1, KEY_BLOCK))

    zero_slab = jnp.zeros((HEAD_DIM, KEY_BLOCK), BF16)
    shift = []
    for h in range(N_HEADS):
        n = h // KV_GROUP
        cols = slice(h * KEY_BLOCK, (h + 1) * KEY_BLOCK)
        q_h = qt_ref[h * HEAD_DIM:(h + 1) * HEAD_DIM, :]
        qbd_ref[n * HEAD_DIM:(n + 1) * HEAD_DIM, cols] = q_h
        qbd_ref[(1 - n) * HEAD_DIM:(2 - n) * HEAD_DIM, cols] = zero_slab
        q_f = q_h.astype(F32)
        shift.append(-jnp.sqrt(jnp.sum(q_f * q_f, axis=0, keepdims=True)) * kmax_ref[n:n + 1, :])
    first_row = lax.broadcasted_iota(I32, (BF16_ROWS, N_HEADS * KEY_BLOCK), 0) == 0
    qbd_ref[FEAT:FEAT + BF16_ROWS, :] = jnp.where(first_row, jnp.concatenate(shift, axis=1), 0.0).astype(BF16)
    qbd_ref[FEAT + BF16_ROWS:, :] = jnp.zeros((FEAT - BF16_ROWS, N_HEADS * KEY_BLOCK), BF16)
    for h in range(N_IDX_HEADS):
        qia_ref[:, h * KEY_BLOCK:(h + 1) * KEY_BLOCK] = qit_ref[h * IDX_DIM:(h + 1) * IDX_DIM, :]

    w = wit_ref[...]
    row_iota = lax.broadcasted_iota(I32, (ck, KEY_BLOCK), 0)
    q_pos = i * KEY_BLOCK + lax.broadcasted_iota(I32, (ck, KEY_BLOCK), 1)

    def score_body(c, carry):
        off = pl.multiple_of(c * ck, ck)
        dots = jnp.dot(kib_ref[pl.ds(off, ck), :], qia_ref[...], preferred_element_type=F32)
        s = _tree_sum(jnp.maximum(dots[:, h * KEY_BLOCK:(h + 1) * KEY_BLOCK], 0.0) * w[h:h + 1, :]
                      for h in range(N_IDX_HEADS))
        keys_ref[pl.ds(off, ck), :] = jnp.where(off + row_iota <= q_pos, _sortable_key(s), INT_MIN)
        return carry

    lax.fori_loop(0, nch, score_body, 0)

    def count_where(pred):
        def body(c, acc):
            off = pl.multiple_of(c * ck, ck)
            hit = pred(keys_ref[pl.ds(off, ck), :], off, slice(0, KEY_BLOCK))
            return acc + _tree_sum(hit[r * SUBLANES:(r + 1) * SUBLANES, :] for r in range(ck // SUBLANES))
        acc = lax.fori_loop(0, nch, body, jnp.zeros((SUBLANES, KEY_BLOCK), I32))
        return jnp.sum(acc.astype(F32), axis=0, keepdims=True)

    def tile_rows(v):
        return jnp.broadcast_to(v, (SUBLANES, KEY_BLOCK))

    def count_ge(t):
        t8 = tile_rows(t)
        return count_where(lambda blk, off, lanes: (blk >= t8[:, lanes][0:1, :]).astype(I32))

    def bit_body(t, thr):
        cand = thr + (jnp.int32(1) << (31 - t))
        return jnp.where(count_ge(cand) >= topk, cand, thr)

    thr_top = lax.fori_loop(0, 32 - LOW_BITS, bit_body, jnp.full((1, KEY_BLOCK), INT_MIN, I32))
    thr_top_c = jnp.maximum(thr_top, INT_MIN + 1)
    cnt_top = count_ge(thr_top_c)
    thr_ref[...] = thr_top_c
    cnt_ref[...] = cnt_top

    @pl.when(jnp.max(cnt_top) > topk)
    def _():
        thr_all = jnp.maximum(lax.fori_loop(32 - LOW_BITS, 32, bit_body, thr_top), INT_MIN + 1)
        thr_ref[...] = thr_all
        cnt_ref[...] = count_ge(thr_all)

    thr = thr_ref[...]
    cnt_ge = cnt_ref[...]

    @pl.when(jnp.max(cnt_ge) > topk)
    def _():
        thr8 = tile_rows(thr)
        cnt_gt = count_where(lambda blk, off, lanes: (blk > thr8[:, lanes][0:1, :]).astype(I32))
        need = topk - cnt_gt

        def idx_body(t, x):
            cand = x + (jnp.int32(1) << (idx_bits - 1 - t))
            cand8 = tile_rows(cand)
            cnt = count_where(lambda blk, off, lanes: jnp.where(
                blk == thr8[:, lanes][0:1, :],
                (off + row_iota[:, lanes] < cand8[:, lanes][0:1, :]).astype(I32), 0))
            return jnp.where(cnt <= need - 1, cand, x)

        x = lax.fori_loop(0, idx_bits, idx_body, jnp.zeros((1, KEY_BLOCK), I32))

        def drop_body(c, carry):
            off = pl.multiple_of(c * ck, ck)
            blk = keys_ref[pl.ds(off, ck), :]
            keys_ref[pl.ds(off, ck), :] = jnp.where(
                blk == thr, jnp.where(off + row_iota > x, INT_MIN, blk), blk)
            return carry

        lax.fori_loop(0, nch, drop_body, 0)

    ones_rows = jnp.ones((BF16_ROWS, ck), BF16)
    acc_ref[...] = jnp.zeros(acc_ref.shape, F32)

    def fast_body(c, carry):
        off = pl.multiple_of(c * ck, ck)
        st = jnp.dot(kb_ref[pl.ds(off, ck), :], qbd_ref[...], preferred_element_type=F32)
        sel = keys_ref[pl.ds(off, ck), :] >= thr
        vt_blk = vt_ref[c]
        for n in range(N_KV_HEADS):
            p_n = jnp.concatenate(
                [jnp.where(sel, jnp.exp2(st[:, h * KEY_BLOCK:(h + 1) * KEY_BLOCK]), 0.0).astype(BF16)
                 for h in range(n * KV_GROUP, (n + 1) * KV_GROUP)], axis=1)
            lhs = jnp.concatenate([vt_blk[n * HEAD_DIM:(n + 1) * HEAD_DIM, :], ones_rows], axis=0)
            acc_ref[n] += jnp.dot(lhs, p_n, preferred_element_type=F32)
        return carry

    lax.fori_loop(0, nch, fast_body, 0)
    l_min = jnp.minimum(jnp.min(acc_ref[0, HEAD_DIM:HEAD_DIM + 1, :]), jnp.min(acc_ref[1, HEAD_DIM:HEAD_DIM + 1, :]))

    def online_body(c, carry):
        off = pl.multiple_of(c * ck, ck)
        st = jnp.dot(kb_ref[pl.ds(off, ck), :], qbd_ref[...], preferred_element_type=F32)
        sel = keys_ref[pl.ds(off, ck), :] >= thr
        vt_blk = vt_ref[c]
        m_old = m_ref[...]
        m_new = []
        for n in range(N_KV_HEADS):
            p_cols, alphas = [], []
            for g in range(KV_GROUP):
                h = n * KV_GROUP + g
                cols = slice(h * KEY_BLOCK, (h + 1) * KEY_BLOCK)
                s_c = jnp.where(sel, st[:, cols], NEG_BIG)
                m_c = jnp.maximum(m_old[:, cols], jnp.max(s_c, axis=0, keepdims=True))
                alphas.append(jnp.exp2(m_old[:, cols] - m_c))
                p_cols.append(jnp.exp2(s_c - m_c).astype(BF16))
                m_new.append(m_c)
            p_n = jnp.concatenate(p_cols, axis=1)
            a_n = jnp.concatenate(alphas, axis=1)
            lhs = jnp.concatenate([vt_blk[n * HEAD_DIM:(n + 1) * HEAD_DIM, :], ones_rows], axis=0)
            acc_ref[n] = a_n * acc_ref[n] + jnp.dot(lhs, p_n, preferred_element_type=F32)
        m_ref[...] = jnp.concatenate(m_new, axis=1)
        return carry

    @pl.when(jnp.logical_not(l_min > UNDERFLOW_GUARD))
    def _():
        m_ref[...] = jnp.full(m_ref.shape, NEG_BIG, F32)
        acc_ref[...] = jnp.zeros(acc_ref.shape, F32)
        lax.fori_loop(0, nch, online_body, 0)

    pieces = []
    for h in range(N_HEADS):
        n, g = divmod(h, KV_GROUP)
        cols = slice(g * KEY_BLOCK, (g + 1) * KEY_BLOCK)
        pieces.append(acc_ref[n, 0:HEAD_DIM, cols] / acc_ref[n, HEAD_DIM:HEAD_DIM + 1, cols])
    out_ref[...] = jnp.transpose(jnp.concatenate(pieces, axis=0)).astype(out_ref.dtype)


def _attn_prompt(qt, qit, wit, kb, vt, kib, batch, seq):
    n = batch * seq
    nq = seq // KEY_BLOCK
    ck = min(ATTN_CHUNK, seq)
    topk = min(TOPK_MAX, seq // 4)
    idx_bits = max(1, int(np.ceil(np.log2(seq))))
    qspec = lambda r: pl.BlockSpec((r, KEY_BLOCK), lambda b, i: (0, b * nq + i))
    kernel = functools.partial(_attn_prompt_kernel, topk=topk, idx_bits=idx_bits, ck=ck)
    return pl.pallas_call(
        kernel,
        grid=(batch, nq),
        in_specs=[qspec(512), qspec(512), qspec(N_IDX_HEADS),
                  pl.BlockSpec((seq, 2 * FEAT), lambda b, i: (b, 0)),
                  pl.BlockSpec((seq // ck, 128, ck), lambda b, i: (b, 0, 0)),
                  pl.BlockSpec((seq, IDX_DIM), lambda b, i: (b, 0))],
        out_specs=pl.BlockSpec((KEY_BLOCK, 512), lambda b, i: (b * nq + i, 0)),
        out_shape=jax.ShapeDtypeStruct((n, 512), BF16),
        scratch_shapes=[pltpu.VMEM((seq, KEY_BLOCK), I32),
                        pltpu.VMEM((1, KEY_BLOCK), I32), pltpu.VMEM((1, KEY_BLOCK), F32),
                        pltpu.VMEM((2 * FEAT, N_HEADS * KEY_BLOCK), BF16),
                        pltpu.VMEM((IDX_DIM, N_IDX_HEADS * KEY_BLOCK), BF16),
                        pltpu.VMEM((N_KV_HEADS, KEY_BLOCK), F32),
                        pltpu.VMEM((1, N_HEADS * KEY_BLOCK), F32),
                        pltpu.VMEM((N_KV_HEADS, HEAD_DIM + BF16_ROWS, KV_GROUP * KEY_BLOCK), F32)],
        compiler_params=pltpu.CompilerParams(dimension_semantics=("arbitrary", "arbitrary"),
                                             vmem_limit_bytes=VMEM_LIMIT),
        name="attn_prompt",
    )(qt, qit, wit, kb, vt, kib)


def _mix_tail(a_bf, diff, x, wpool_ref, pscale_ref, wout_ref, gpost_ref, gpre_ref, x1_ref, h2_ref):
    outs = []
    for g in range(len(POOL_WINDOWS)):
        cols = slice(g * POOL_GROUP_DIM, (g + 1) * POOL_GROUP_DIM)
        outs.append(jnp.dot(diff[:, cols].astype(BF16), wpool_ref[g], preferred_element_type=F32))
    m = jnp.concatenate(outs, axis=1) * pscale_ref[...]
    mix_in = jnp.concatenate([a_bf, m.astype(BF16)], axis=1)
    mix = jnp.dot(mix_in, wout_ref[...], preferred_element_type=F32)
    x1 = x + _rms(mix, gpost_ref[...])
    x1_ref[...] = x1
    h2_ref[...] = _rms(x1, gpre_ref[...]).astype(BF16)


def _mix_prompt_kernel(a_ref, xp_ref, x_ref, wpool_ref, pscale_ref, wout_ref, gpost_ref, gpre_ref,
                       x1_ref, h2_ref, prev_ref, *, tm, nt):
    i = pl.program_id(0)
    halo = POOL_BUF + 1

    @pl.when(i % nt == 0)
    def _():
        prev_ref[...] = jnp.zeros(prev_ref.shape, F32)

    xp = xp_ref[...]
    ext = jnp.concatenate([prev_ref[...], xp], axis=0)
    prev_ref[...] = xp[tm - halo:, :]
    t = (i % nt) * tm + lax.broadcasted_iota(I32, (tm, 1), 0)
    diffs = []
    for g, win in enumerate(POOL_WINDOWS):
        cols = slice(g * POOL_GROUP_DIM, (g + 1) * POOL_GROUP_DIM)
        s = ext[:, cols]
        sh = 1
        while sh < win:
            s = s + pltpu.roll(s, sh, 0)
            sh *= 2
        cnt = jnp.minimum(t + 1, win).astype(F32)
        diffs.append(s[halo:, :] / cnt - xp[:, cols])
    diff = jnp.concatenate(diffs, axis=1)
    _mix_tail(a_ref[...], diff, x_ref[...], wpool_ref, pscale_ref, wout_ref, gpost_ref, gpre_ref,
              x1_ref, h2_ref)


def _mix_prompt(a, xp, x2d, wpool, pscale, wout, gpost, gpre, seq):
    n, d = x2d.shape
    tm = min(SEQ_TILE, seq)
    nt = seq // tm
    row = lambda w: pl.BlockSpec((tm, w), lambda i: (i, 0))
    return pl.pallas_call(
        functools.partial(_mix_prompt_kernel, tm=tm, nt=nt),
        grid=(n // tm,),
        in_specs=[row(512), row(512), row(d), _const_spec(wpool.shape), _const_spec((1, 512)),
                  _const_spec(wout.shape), _const_spec((1, d)), _const_spec((1, d))],
        out_specs=(row(d), row(d)),
        out_shape=(jax.ShapeDtypeStruct((n, d), F32), jax.ShapeDtypeStruct((n, d), BF16)),
        scratch_shapes=[pltpu.VMEM((POOL_BUF + 1, 512), F32)],
        compiler_params=pltpu.CompilerParams(dimension_semantics=("arbitrary",), vmem_limit_bytes=VMEM_LIMIT),
        name="mix_prompt",
    )(a, xp, x2d, wpool, pscale, wout, gpost, gpre)


def _mix_sample_kernel(a_ref, ext_ref, x_ref, wpool_ref, pscale_ref, wout_ref, gpost_ref, gpre_ref,
                       x1_ref, h2_ref):
    rows = POOL_BUF + 1
    last = ext_ref[rows - 1]
    diffs = []
    for g, win in enumerate(POOL_WINDOWS):
        cols = slice(g * POOL_GROUP_DIM, (g + 1) * POOL_GROUP_DIM)
        wsum = last[:, cols]
        for r in range(rows - win, rows - 1):
            wsum = wsum + ext_ref[r][:, cols]
        diffs.append(wsum / float(win) - last[:, cols])
    diff = jnp.concatenate(diffs, axis=1)
    _mix_tail(a_ref[...].astype(BF16), diff, x_ref[...], wpool_ref, pscale_ref, wout_ref, gpost_ref,
              gpre_ref, x1_ref, h2_ref)


def _mix_sample(a, ext, x2d, wpool, pscale, wout, gpost, gpre):
    n, d = x2d.shape
    full = lambda s: pl.BlockSpec(s, lambda: (0,) * len(s))
    return pl.pallas_call(
        _mix_sample_kernel,
        in_specs=[full(a.shape), full(ext.shape), full(x2d.shape), full(wpool.shape), full((1, 512)),
                  full(wout.shape), full((1, d)), full((1, d))],
        out_specs=(full((n, d)), full((n, d))),
        out_shape=(jax.ShapeDtypeStruct((n, d), F32), jax.ShapeDtypeStruct((n, d), BF16)),
        compiler_params=pltpu.CompilerParams(vmem_limit_bytes=VMEM_LIMIT),
        name="mix_sample",
    )(a, ext, x2d, wpool, pscale, wout, gpost, gpre)


def _ffn_tail(c, val, x1, wdown_ref, gpost_ref, x2_ref):
    y = (jax.nn.gelu(c, approximate=True) * val).astype(BF16)
    f = jnp.dot(y, wdown_ref[...], preferred_element_type=F32)
    x2_ref[...] = x1 + _rms(f, gpost_ref[...])


def _ffn_prompt_kernel(h2_ref, x1_ref, wg_ref, wv_ref, cw_ref, wdown_ref, gpost_ref,
                       x2_ref, cst_ref, prev_ref, *, tm, nt):
    i = pl.program_id(0)

    @pl.when(i % nt == 0)
    def _():
        prev_ref[...] = jnp.zeros(prev_ref.shape, F32)

    h2 = h2_ref[...]
    g = jnp.dot(h2, wg_ref[...], preferred_element_type=F32)
    val = jnp.dot(h2, wv_ref[...], preferred_element_type=F32)
    ext = jnp.concatenate([prev_ref[...], g], axis=0)
    tail = g[tm - SUBLANES:, :]
    prev_ref[...] = tail
    cst_ref[0] = tail
    cw = cw_ref[...]
    g1 = pltpu.roll(ext, 1, 0)[SUBLANES:, :]
    g2 = pltpu.roll(ext, 2, 0)[SUBLANES:, :]
    c = cw[3:4, :] + g2 * cw[0:1, :] + g1 * cw[1:2, :] + g * cw[2:3, :]
    _ffn_tail(c, val, x1_ref[...], wdown_ref, gpost_ref, x2_ref)


def _ffn_prompt(h2, x1, wg, wv, cw, wdown, gpost, batch, seq):
    n, d = x1.shape
    f = wg.shape[1]
    tm = min(256, seq)
    nt = seq // tm
    row = lambda w: pl.BlockSpec((tm, w), lambda i: (i, 0))
    return pl.pallas_call(
        functools.partial(_ffn_prompt_kernel, tm=tm, nt=nt),
        grid=(n // tm,),
        in_specs=[row(d), row(d), _const_spec(wg.shape), _const_spec(wv.shape), _const_spec(cw.shape),
                  _const_spec(wdown.shape), _const_spec((1, d))],
        out_specs=(row(d), pl.BlockSpec((1, SUBLANES, f), lambda i: (i // nt, 0, 0))),
        out_shape=(jax.ShapeDtypeStruct((n, d), F32), jax.ShapeDtypeStruct((batch, SUBLANES, f), F32)),
        scratch_shapes=[pltpu.VMEM((SUBLANES, f), F32)],
        compiler_params=pltpu.CompilerParams(dimension_semantics=("arbitrary",), vmem_limit_bytes=VMEM_LIMIT),
        name="ffn_prompt",
    )(h2, x1, wg, wv, cw, wdown, gpost)


def _ffn_sample_kernel(h2_ref, x1_ref, b0_ref, b1_ref, wg_ref, wv_ref, cw_ref, wdown_ref, gpost_ref,
                       x2_ref, g_ref):
    h2 = h2_ref[...]
    g = jnp.dot(h2, wg_ref[...], preferred_element_type=F32)
    val = jnp.dot(h2, wv_ref[...], preferred_element_type=F32)
    g_ref[...] = g
    cw = cw_ref[...]
    c = cw[3:4, :] + b0_ref[...] * cw[0:1, :] + b1_ref[...] * cw[1:2, :] + g * cw[2:3, :]
    _ffn_tail(c, val, x1_ref[...], wdown_ref, gpost_ref, x2_ref)


def _ffn_sample(h2, x1, b0, b1, wg, wv, cw, wdown, gpost):
    n, d = x1.shape
    f = wg.shape[1]
    full = lambda s: pl.BlockSpec(s, lambda: (0,) * len(s))
    args = (h2, x1, b0, b1, wg, wv, cw, wdown, gpost)
    return pl.pallas_call(
        _ffn_sample_kernel,
        in_specs=[full(a.shape) for a in args],
        out_specs=(full((n, d)), full((n, f))),
        out_shape=(jax.ShapeDtypeStruct((n, d), F32), jax.ShapeDtypeStruct((n, f), F32)),
        compiler_params=pltpu.CompilerParams(vmem_limit_bytes=VMEM_LIMIT),
        name="ffn_sample",
    )(*args)


def _inproj_sample_kernel(x_ref, g_ref, w_ref, c_ref, sa_ref, sb_ref,
                          q_ref, k_ref, v_ref, qi_ref, kiw_ref, xp_ref):
    h = _rms(x_ref[...], g_ref[...]).astype(BF16)
    z = jnp.dot(h, w_ref[...], preferred_element_type=F32)
    c, sa, sb = c_ref[...], sa_ref[...], sb_ref[...]
    rope = lambda lo: _rope_rows(z[:, lo:lo + LANES], c, sa, sb)
    q_ref[...] = jnp.concatenate([rope(j * LANES) for j in range(4)], axis=1) * (HEAD_DIM ** -0.5)
    k_ref[...] = rope(512)
    v_ref[...] = z[:, 640:768]
    qi_ref[...] = jnp.concatenate([rope(768 + j * LANES) for j in range(4)], axis=1)
    lane = lax.broadcasted_iota(I32, (1, LANES), 1)
    kiw = z[:, 1280:1408]
    kiw_ref[...] = jnp.where(lane < IDX_DIM, _rope_rows(kiw, c, sa, sb), kiw)
    xp_ref[...] = z[:, 1408:1920]


def _inproj_sample(x2d, g, wall, tabs):
    n, d = x2d.shape
    full = lambda s: pl.BlockSpec(s, lambda: (0,) * len(s))
    args = (x2d, g, wall, tabs["c"], tabs["sa"], tabs["sb"])
    widths = (512, 128, 128, 512, 128, 512)
    return pl.pallas_call(
        _inproj_sample_kernel,
        in_specs=[full(a.shape) for a in args],
        out_specs=tuple(full((n, w)) for w in widths),
        out_shape=tuple(jax.ShapeDtypeStruct((n, w), F32) for w in widths),
        compiler_params=pltpu.CompilerParams(vmem_limit_bytes=VMEM_LIMIT),
        name="inproj_sample",
    )(*args)


def _page_copy(cache_ref, buf_ref, sem_ref, layer, page, slot, j):
    return pltpu.make_async_copy(cache_ref.at[layer, page], buf_ref.at[slot, j], sem_ref.at[slot])


def _chunk_dma(pt_ref, caches, bufs, sems, layer, b, c, slot, cp, start):
    for j in range(cp):
        page = pt_ref[b, c * cp + j]
        for cache_ref, buf_ref, sem_ref in zip(caches, bufs, sems):
            desc = _page_copy(cache_ref, buf_ref, sem_ref, layer, page, slot, j)
            if start:
                desc.start()
            else:
                desc.wait()


def _pipeline_step(pt_ref, caches, bufs, sems, layer, cp):
    b, c = pl.program_id(0), pl.program_id(1)
    nb, nc = pl.num_programs(0), pl.num_programs(1)
    step = b * nc + c
    slot = step % 2

    @pl.when(step == 0)
    def _():
        _chunk_dma(pt_ref, caches, bufs, sems, layer, b, c, slot, cp, True)

    @pl.when(step + 1 < nb * nc)
    def _():
        nxt = step + 1
        _chunk_dma(pt_ref, caches, bufs, sems, layer, nxt // nc, nxt % nc, 1 - slot, cp, True)

    _chunk_dma(pt_ref, caches, bufs, sems, layer, b, c, slot, cp, False)
    return slot


def _chunk_operand(buf_ref, slot, cp):
    return jnp.concatenate([buf_ref[slot, j] for j in range(cp)], axis=1).astype(BF16)


def _decode_score_kernel(pt_ref, qi_ref, w_ref, kin_ref, cache_ref, keys_ref, knew_ref,
                         buf_ref, sem_ref, *, layer, cp):
    c = pl.program_id(1)
    slot = _pipeline_step(pt_ref, (cache_ref,), (buf_ref,), (sem_ref,), layer, cp)

    qi = qi_ref[0]
    w8 = w_ref[0][:, 0:1]
    dots = jnp.dot(qi, _chunk_operand(buf_ref, slot, cp), preferred_element_type=F32)
    s = jnp.sum(jnp.maximum(dots, 0.0) * w8, axis=0, keepdims=True)
    keys_ref[0, pl.ds(c, 1), :] = _sortable_key(s)

    @pl.when(c == 0)
    def _():
        ki_new = kin_ref[0][:, 0:IDX_DIM].astype(BF16).astype(F32)
        d_new = jnp.sum(qi.astype(F32) * ki_new, axis=1, keepdims=True)
        s_new = jnp.sum(jnp.maximum(d_new, 0.0) * w8, axis=0, keepdims=True)
        knew_ref[0] = jnp.broadcast_to(_sortable_key(s_new), (1, LANES))


def _decode_select_kernel(keys_ref, knew_ref, bias_ref, bnew_ref, *, topk, idx_bits, past):
    key_new = knew_ref[...][:, 0:1]
    one = lambda pred: jnp.where(pred, 1.0, 0.0)
    total = lambda fk, fn: jnp.sum(fk, axis=1, keepdims=True) + fn

    def bit_body(t, thr):
        cand = thr + (jnp.int32(1) << (31 - t))
        return jnp.where(total(one(keys_ref[...] >= cand), one(key_new >= cand)) >= topk, cand, thr)

    thr = lax.fori_loop(0, 32, bit_body, jnp.full(key_new.shape, INT_MIN, I32))
    thr = jnp.maximum(thr, INT_MIN + 1)
    need = topk - total(one(keys_ref[...] > thr), one(key_new > thr))

    idx = lax.broadcasted_iota(I32, keys_ref.shape, 1)

    def idx_body(t, x):
        cand = x + (jnp.int32(1) << (idx_bits - 1 - t))
        cnt = total(jnp.where(keys_ref[...] == thr, one(idx < cand), 0.0),
                    jnp.where(key_new == thr, one(past < cand), 0.0))
        return jnp.where(cnt <= need - 1.0, cand, x)

    x = lax.fori_loop(0, idx_bits, idx_body, jnp.zeros(key_new.shape, I32))
    keys = keys_ref[...]
    sel = jnp.where(keys == thr, one(idx <= x), one(keys > thr))
    sel_n = jnp.where(key_new == thr, one(past <= x), one(key_new > thr))
    bias_ref[...] = (1.0 - sel) * NEG_BIG
    bnew_ref[...] = jnp.broadcast_to((1.0 - sel_n) * NEG_BIG, bnew_ref.shape)


def _decode_attend_kernel(pt_ref, q_ref, bias_ref, bnew_ref, kn_ref, vn_ref, ck_ref, cv_ref, out_ref,
                          kbuf_ref, vbuf_ref, ksem_ref, vsem_ref, m_ref, l_ref, acc_ref, *, layer, cp):
    c = pl.program_id(1)
    nc = pl.num_programs(1)
    slot = _pipeline_step(pt_ref, (ck_ref, cv_ref), (kbuf_ref, vbuf_ref), (ksem_ref, vsem_ref), layer, cp)
    q = q_ref[0]

    @pl.when(c == 0)
    def _():
        k_new = kn_ref[0].astype(BF16).astype(F32)
        s_new = jnp.sum(q.astype(F32) * k_new, axis=1, keepdims=True) + bnew_ref[0][:, 0:1]
        m_ref[...] = jnp.broadcast_to(s_new, m_ref.shape)
        l_ref[...] = jnp.ones(l_ref.shape, F32)
        acc_ref[...] = jnp.broadcast_to(vn_ref[0].astype(BF16).astype(F32), acc_ref.shape)

    kt = _chunk_operand(kbuf_ref, slot, cp)
    vt = _chunk_operand(vbuf_ref, slot, cp)
    s = jnp.dot(q, kt, preferred_element_type=F32) + bias_ref[0, pl.ds(c, 1), :]
    m_old = m_ref[...][:, 0:1]
    m_new = jnp.maximum(m_old, jnp.max(s, axis=1, keepdims=True))
    alpha = jnp.exp(m_old - m_new)
    p = jnp.exp(s - m_new)
    l_ref[...] = alpha * l_ref[...] + jnp.sum(p, axis=1, keepdims=True)
    pv = lax.dot_general(p.astype(BF16), vt, (((1,), (1,)), ((), ())), preferred_element_type=F32)
    acc_ref[...] = alpha * acc_ref[...] + pv
    m_ref[...] = jnp.broadcast_to(m_new, m_ref.shape)

    @pl.when(c == nc - 1)
    def _():
        o = acc_ref[...] / l_ref[...]
        head = lax.broadcasted_iota(I32, (N_HEADS, HEAD_DIM), 0)
        out_ref[0] = jnp.where(head < KV_GROUP, o[:, 0:HEAD_DIM], o[:, HEAD_DIM:])


def _decode_attention(page_table, layer, q, qi, kiw, k_new, v_new, ckt, cvt, cikt):
    nb, n_pages = page_table.shape
    past = n_pages * PAGE_SIZE
    topk = min(TOPK_MAX, (past + 1) // 4)
    idx_bits = int(np.floor(np.log2(past))) + 1
    cp = min(DECODE_CHUNK_PAGES, n_pages)
    nc = n_pages // cp
    cpk = cp * PAGE_SIZE

    q3 = q.reshape(nb, N_KV_HEADS, KV_GROUP, HEAD_DIM)
    zq = jnp.zeros((nb, KV_GROUP, HEAD_DIM), F32)
    q_bd = jnp.concatenate([jnp.concatenate([q3[:, 0], zq], axis=2),
                            jnp.concatenate([zq, q3[:, 1]], axis=2)], axis=1).astype(BF16)
    qi3 = qi.reshape(nb, N_IDX_HEADS, IDX_DIM).astype(BF16)
    wi = kiw[:, IDX_DIM:IDX_DIM + N_IDX_HEADS] * ((IDX_DIM ** -0.5) * (N_IDX_HEADS ** -0.5))
    w8 = jnp.broadcast_to(wi[:, :, None], (nb, N_IDX_HEADS, LANES))
    kin3, kn3, vn3 = kiw[:, None, :], k_new[:, None, :], v_new[:, None, :]

    seq_blk = lambda s: pl.BlockSpec((1,) + s, lambda b, c, pt: (b,) + (0,) * len(s))
    any_spec = pl.BlockSpec(memory_space=pl.ANY)
    params = pltpu.CompilerParams(dimension_semantics=("arbitrary", "arbitrary"), vmem_limit_bytes=VMEM_LIMIT)

    keys, knew = pl.pallas_call(
        functools.partial(_decode_score_kernel, layer=layer, cp=cp),
        grid_spec=pltpu.PrefetchScalarGridSpec(
            num_scalar_prefetch=1, grid=(nb, nc),
            in_specs=[seq_blk((N_IDX_HEADS, IDX_DIM)), seq_blk((N_IDX_HEADS, LANES)), seq_blk((1, LANES)),
                      any_spec],
            out_specs=(seq_blk((nc, cpk)), seq_blk((1, LANES))),
            scratch_shapes=[pltpu.VMEM((2, cp, IDX_DIM, PAGE_SIZE), F32), pltpu.SemaphoreType.DMA((2,))]),
        out_shape=(jax.ShapeDtypeStruct((nb, nc, cpk), I32), jax.ShapeDtypeStruct((nb, 1, LANES), I32)),
        compiler_params=params, name="decode_score",
    )(page_table, qi3, w8, kin3, cikt)

    full = lambda s: pl.BlockSpec(s, lambda: (0,) * len(s))
    bias, bnew = pl.pallas_call(
        functools.partial(_decode_select_kernel, topk=topk, idx_bits=idx_bits, past=past),
        in_specs=[full((nb, past)), full((nb, LANES))],
        out_specs=(full((nb, past)), full((nb, LANES))),
        out_shape=(jax.ShapeDtypeStruct((nb, past), F32), jax.ShapeDtypeStruct((nb, LANES), F32)),
        compiler_params=pltpu.CompilerParams(vmem_limit_bytes=VMEM_LIMIT),
        name="decode_select",
    )(keys.reshape(nb, past), knew.reshape(nb, LANES))
    bias, bnew = bias.reshape(nb, nc, cpk), bnew.reshape(nb, 1, LANES)

    feat = N_KV_HEADS * HEAD_DIM
    out = pl.pallas_call(
        functools.partial(_decode_attend_kernel, layer=layer, cp=cp),
        grid_spec=pltpu.PrefetchScalarGridSpec(
            num_scalar_prefetch=1, grid=(nb, nc),
            in_specs=[seq_blk((N_HEADS, feat)), seq_blk((nc, cpk)), seq_blk((1, LANES)),
                      seq_blk((1, LANES)), seq_blk((1, LANES)), any_spec, any_spec],
            out_specs=seq_blk((N_HEADS, HEAD_DIM)),
            scratch_shapes=[pltpu.VMEM((2, cp, feat, PAGE_SIZE), F32), pltpu.VMEM((2, cp, feat, PAGE_SIZE), F32),
                            pltpu.SemaphoreType.DMA((2,)), pltpu.SemaphoreType.DMA((2,)),
                            pltpu.VMEM((N_HEADS, LANES), F32), pltpu.VMEM((N_HEADS, LANES), F32),
                            pltpu.VMEM((N_HEADS, feat), F32)]),
        out_shape=jax.ShapeDtypeStruct((nb, N_HEADS, HEAD_DIM), F32),
        compiler_params=params, name="decode_attend",
    )(page_table, q_bd, bias, bnew, kn3, vn3, ckt, cvt)
    return out.reshape(nb, N_HEADS * HEAD_DIM)


def _prep_layer(w_in, w_pool, pool_scale, w_out, w_gate, w_val, conv_w, conv_b, w_down,
                g_pre_mix, g_post_mix, g_pre_ffn, g_post_ffn):
    d = w_in.shape[0]
    sizes = (512, 128, 128, 512, IDX_DIM, N_IDX_HEADS, 512)
    cuts = np.cumsum(sizes)[:-1]
    wq, wk, wv, wqi, wki, wwi, wxp = jnp.split(w_in, [int(c) for c in cuts], axis=1)
    pad = lambda n: jnp.zeros((d, n), w_in.dtype)
    f = w_gate.shape[1]
    cw = jnp.concatenate([conv_w, conv_b[None, :], jnp.zeros((SUBLANES - CONV_WIDTH - 1, f), F32)], axis=0)
    return dict(
        wrow=jnp.concatenate([wk, wv, wki, pad(LANES - IDX_DIM), wxp], axis=1).astype(BF16),
        wt=jnp.concatenate([wq, wqi, wv, wwi, pad(BF16_ROWS - N_IDX_HEADS)], axis=1).T.astype(BF16),
        wall=jnp.concatenate([wq, wk, wv, wqi, wki, wwi, pad(LANES - IDX_DIM - N_IDX_HEADS), wxp],
                             axis=1).astype(BF16),
        wpool=w_pool.astype(BF16), pscale=pool_scale[None, :], wout=w_out.astype(BF16),
        wg=w_gate.astype(BF16), wv=w_val.astype(BF16), cw=cw, wdown=w_down.astype(BF16),
        g_pre_mix=g_pre_mix[None, :], g_post_mix=g_post_mix[None, :],
        g_pre_ffn=g_pre_ffn[None, :], g_post_ffn=g_post_ffn[None, :])


def kernel(x_prompt, x_sample, cache_k, cache_v, cache_idx_k, state_pool, state_conv, page_table,
           g_pre_mix, w_in, w_pool, pool_scale, w_out, g_post_mix, g_pre_ffn,
           w_gate, w_val, conv_w, conv_b, w_down, g_post_ffn):
    bp, sp, d = x_prompt.shape
    bs, ts, _ = x_sample.shape
    assert ts == 1 and sp % KEY_BLOCK == 0 and sp >= 2 * KEY_BLOCK
    depth, n_phys = cache_k.shape[:2]
    past = page_table.shape[1] * PAGE_SIZE
    tabs_p = _rope_tables(jnp.arange(sp))
    tabs_s = _rope_tables(jnp.full((bs,), past, I32))
    feat = N_KV_HEADS * HEAD_DIM
    ckt = jnp.transpose(cache_k, (0, 1, 3, 4, 2)).reshape(depth, n_phys, feat, PAGE_SIZE)
    cvt = jnp.transpose(cache_v, (0, 1, 3, 4, 2)).reshape(depth, n_phys, feat, PAGE_SIZE)
    cikt = jnp.transpose(cache_idx_k, (0, 1, 3, 2))

    yp = x_prompt.reshape(bp * sp, d)
    ys = x_sample.reshape(bs, d)
    outs = [[] for _ in range(10)]
    for l in range(depth):
        p = _prep_layer(w_in[l], w_pool[l], pool_scale[l], w_out[l], w_gate[l], w_val[l], conv_w[l],
                        conv_b[l], w_down[l], g_pre_mix[l], g_post_mix[l], g_pre_ffn[l], g_post_ffn[l])
        k, v, ki, xp, kb, kib, qt, qit, vt, wit = _inproj_prompt(yp, p["g_pre_mix"], p["wrow"], p["wt"], tabs_p, sp)
        a = _attn_prompt(qt, qit, wit, kb, vt, kib, bp, sp)
        x1, h2 = _mix_prompt(a, xp, yp, p["wpool"], p["pscale"], p["wout"], p["g_post_mix"], p["g_pre_ffn"], sp)
        yp, cst = _ffn_prompt(h2, x1, p["wg"], p["wv"], p["cw"], p["wdown"], p["g_post_ffn"], bp, sp)
        outs[0].append(k.reshape(bp, sp, N_KV_HEADS, HEAD_DIM))
        outs[1].append(v.reshape(bp, sp, N_KV_HEADS, HEAD_DIM))
        outs[2].append(ki.reshape(bp, sp, IDX_DIM))
        outs[3].append(xp.reshape(bp, sp, 512)[:, sp - POOL_BUF:, :])
        outs[4].append(cst[:, SUBLANES - (CONV_WIDTH - 1):, :])
        q_s, k_s, v_s, qi_s, kiw_s, xp_s = _inproj_sample(ys, p["g_pre_mix"], p["wall"], tabs_s)
        a_s = _decode_attention(page_table, l, q_s, qi_s, kiw_s, k_s, v_s, ckt, cvt, cikt)
        ext = jnp.concatenate([state_pool[l], xp_s[:, None, :]], axis=1)
        x1_s, h2_s = _mix_sample(a_s, jnp.transpose(ext, (1, 0, 2)), ys, p["wpool"], p["pscale"], p["wout"],
                                 p["g_post_mix"], p["g_pre_ffn"])
        ys, g_s = _ffn_sample(h2_s, x1_s, state_conv[l, :, 0, :], state_conv[l, :, 1, :],
                              p["wg"], p["wv"], p["cw"], p["wdown"], p["g_post_ffn"])
        outs[5].append(k_s.reshape(bs, 1, N_KV_HEADS, HEAD_DIM))
        outs[6].append(v_s.reshape(bs, 1, N_KV_HEADS, HEAD_DIM))
        outs[7].append(kiw_s[:, None, :IDX_DIM])
        outs[8].append(ext[:, 1:, :])
        outs[9].append(jnp.concatenate([state_conv[l, :, 1:, :], g_s[:, None, :]], axis=1))
    return (yp.reshape(bp, sp, d), ys.reshape(bs, 1, d)) + tuple(jnp.stack(o) for o in outs)
```
